```python
import math
import jax
import jax.numpy as jnp
from jax import lax
import numpy as np

D_MODEL = 2048
BATCH = 4
SEQ = 2048
DEPTH = 1

NORM_EPS = 1e-5
DILATED_CONFIGS = ((128, 1), (512, 4), (2048, 16))
N_ATTN_GROUPS = 3
ATTN_HEADS_PER_GROUP = 8
ATTN_HEAD_DIM = 128
ATTN_WIDTH = N_ATTN_GROUPS * ATTN_HEADS_PER_GROUP * ATTN_HEAD_DIM
ATTN_OUT_WIDTH = ATTN_HEADS_PER_GROUP * ATTN_HEAD_DIM
ATTN_BLOCK = 128
ROPE_DIMS = ATTN_HEAD_DIM // 4
ROPE_THETA = 500000.0
SSD_INNER = 2 * D_MODEL
SSD_HEAD_DIM = 64
SSD_HEADS = SSD_INNER // SSD_HEAD_DIM
SSD_GROUPS = 8
SSD_HEADS_PER_GROUP = SSD_HEADS // SSD_GROUPS
SSD_STATE = 128
SSD_CONV = 4
SSD_CHUNK = 128
SSD_CONV_CH = SSD_INNER + 2 * SSD_GROUPS * SSD_STATE
N_BRANCHES = 2
SPLIT_SIZES = (ATTN_WIDTH, ATTN_WIDTH, ATTN_WIDTH, ATTN_OUT_WIDTH, SSD_INNER, SSD_CONV_CH, SSD_HEADS, N_BRANCHES * D_MODEL)
D_IN_PROJ = sum(SPLIT_SIZES)

kernel_name = "hybrid_dilated_attn_ssd_gated_block"


def rms_norm(x, w):
    xf = x.astype(jnp.float32)
    xf = xf * lax.rsqrt(jnp.mean(xf * xf, axis=-1, keepdims=True) + NORM_EPS)
    return (xf * w.astype(jnp.float32)).astype(x.dtype)


def partial_rope(t, positions):
    half = ROPE_DIMS // 2
    inv_freq = ROPE_THETA ** (-jnp.arange(half, dtype=jnp.float32) / half)
    ang = positions.astype(jnp.float32)[..., None] * inv_freq
    cos = jnp.cos(ang)[:, :, None, None, :]
    sin = jnp.sin(ang)[:, :, None, None, :]
    tf = t.astype(jnp.float32)
    x1 = tf[..., :half]
    x2 = tf[..., half:ROPE_DIMS]
    out = jnp.concatenate([x1 * cos - x2 * sin, x2 * cos + x1 * sin, tf[..., ROPE_DIMS:]], axis=-1)
    return out.astype(t.dtype)


def dilated_group_attention(q, k, v, window, dilation):
    b, s, h, e = q.shape
    n_sub = window // dilation
    blk = ATTN_BLOCK
    span = dilation * blk
    sp = -(-s // span) * span
    nb = sp // span

    def to_blocks(t):
        t = jnp.pad(t, ((0, 0), (0, sp - s), (0, 0), (0, 0)))
        t = t.reshape(b, nb * blk, dilation, h, e).transpose(0, 2, 1, 3, 4)
        return t.reshape(b, dilation, nb, blk, h, e)

    def with_prev_block(t):
        prev = jnp.pad(t[:, :, :-1], ((0, 0), (0, 0), (1, 0), (0, 0), (0, 0), (0, 0)))
        return jnp.concatenate([prev, t], axis=3)

    qb = to_blocks(q)
    kc = with_prev_block(to_blocks(k))
    vc = with_prev_block(to_blocks(v))
    scale = ATTN_HEAD_DIM ** -0.5
    scores = jnp.einsum("brnqhe,brnkhe->brnhqk", qb, kc).astype(jnp.float32) * scale
    qi = jnp.arange(blk)[:, None]
    ki = jnp.arange(2 * blk)[None, :]
    dist = qi + blk - ki
    band = (dist >= 0) & (dist <= n_sub)
    has_prev = (jnp.arange(nb)[:, None, None] > 0) | (ki >= blk)[None]
    valid = band[None] & has_prev
    scores = jnp.where(valid[None, None, :, None], scores, -jnp.inf)
    m = jnp.max(scores, axis=-1, keepdims=True)
    p = jnp.exp(scores - m)
    den = jnp.sum(p, axis=-1)
    o = jnp.einsum("brnhqk,brnkhe->brnqhe", p, vc.astype(jnp.float32)) / jnp.swapaxes(den, -1, -2)[..., None]
    lse = jnp.swapaxes(m[..., 0] + jnp.log(den), -1, -2)

    def from_blocks(t):
        tail = t.shape[5:]
        t = t.reshape((b, dilation, nb * blk, h) + tail)
        t = jnp.moveaxis(t, 1, 2).reshape((b, sp, h) + tail)
        return t[:, :s]

    return from_blocks(o), from_blocks(lse)


def dilated_mixture_attention(q, k, v):
    b, s = q.shape[:2]
    outs, lses = [], []
    for g, (window, dilation) in enumerate(DILATED_CONFIGS):
        o, l = dilated_group_attention(q[:, :, g], k[:, :, g], v[:, :, g], window, dilation)
        outs.append(o)
        lses.append(l)
    w = jax.nn.softmax(jnp.stack(lses, axis=0), axis=0)
    out = jnp.sum(w[..., None] * jnp.stack(outs, axis=0), axis=0)
    return out.reshape(b, s, ATTN_OUT_WIDTH)


def causal_depthwise_conv(x, w, bias):
    y = lax.conv_general_dilated(
        x, w[:, None, :].astype(x.dtype), window_strides=(1,), padding=[(SSD_CONV - 1, 0)],
        dimension_numbers=("NWC", "WIO", "NWC"), feature_group_count=x.shape[-1])
    return y + bias.astype(x.dtype)


def segsum_exp(a):
    cs = jnp.cumsum(a, axis=-1)
    diff = cs[..., :, None] - cs[..., None, :]
    l = a.shape[-1]
    mask = jnp.tril(jnp.ones((l, l), dtype=bool))
    return jnp.exp(jnp.where(mask, diff, -jnp.inf))


def ssd_mixer(xbc, z, dt_raw, conv_w, conv_b, dt_bias, a_log, d_skip, norm_w):
    b, s, _ = xbc.shape
    G, J, P, N, L = SSD_GROUPS, SSD_HEADS_PER_GROUP, SSD_HEAD_DIM, SSD_STATE, SSD_CHUNK
    c = s // L
    xbc = jax.nn.silu(causal_depthwise_conv(xbc, conv_w, conv_b)).astype(jnp.float32)
    xs, bm, cm = jnp.split(xbc, [SSD_INNER, SSD_INNER + G * N], axis=-1)
    xs = xs.reshape(b, c, L, G, J, P)
    bm = bm.reshape(b, c, L, G, N)
    cm = cm.reshape(b, c, L, G, N)
    dt = jax.nn.softplus(dt_raw.astype(jnp.float32) + dt_bias.astype(jnp.float32)).reshape(b, c, L, G, J)
    a = -jnp.exp(a_log.astype(jnp.float32)).reshape(G, J)
    xdt = xs * dt[..., None]
    da = jnp.transpose(dt * a, (0, 1, 3, 4, 2))
    cs = jnp.cumsum(da, axis=-1)
    decay_in = segsum_exp(da)
    cb = jnp.einsum("bclgn,bcsgn->bcgls", cm, bm)
    y_diag = jnp.einsum("bcgls,bcgjls,bcsgjp->bclgjp", cb, decay_in, xdt)
    decay_to_end = jnp.exp(cs[..., -1:] - cs)
    chunk_states = jnp.einsum("bclgn,bcgjl,bclgjp->bcgjpn", bm, decay_to_end, xdt)
    chunk_decay = jnp.exp(cs[..., -1])

    def step(state, inp):
        decay, st = inp
        return state * decay[..., None, None] + st, state

    init = jnp.zeros((b, G, J, P, N), jnp.float32)
    _, prev = lax.scan(step, init, (jnp.moveaxis(chunk_decay, 1, 0), jnp.moveaxis(chunk_states, 1, 0)))
    prev = jnp.moveaxis(prev, 0, 1)
    y_off = jnp.einsum("bclgn,bcgjpn,bcgjl->bclgjp", cm, prev, jnp.exp(cs))
    y = y_diag + y_off + xs * d_skip.astype(jnp.float32).reshape(G, J, 1)
    y = y.reshape(b, s, SSD_INNER) * jax.nn.silu(z.astype(jnp.float32))
    return rms_norm(y, norm_w)


def setup_inputs(seed: int = 0) -> dict:
    key = jax.random.key(seed)
    ks = jax.random.split(key, 16)
    f32 = jnp.float32
    x = jax.random.normal(ks[0], (BATCH, SEQ, D_MODEL), f32)
    positions = (jnp.arange(SEQ, dtype=jnp.int32)[None, :]
                 + jax.random.randint(ks[1], (BATCH, 1), 0, 4096, dtype=jnp.int32))
    norm_w = 1.0 + 0.02 * jax.random.normal(ks[2], (DEPTH, D_MODEL), f32)
    w_in = jax.random.normal(ks[3], (DEPTH, D_MODEL, D_IN_PROJ), f32) * D_MODEL ** -0.5
    conv_w = jax.random.normal(ks[4], (DEPTH, SSD_CONV, SSD_CONV_CH), f32) * SSD_CONV ** -0.5
    conv_b = 0.02 * jax.random.normal(ks[5], (DEPTH, SSD_CONV_CH), f32)
    dt0 = jnp.exp(jax.random.uniform(ks[6], (DEPTH, SSD_HEADS), f32, math.log(1e-3), math.log(1e-1)))
    dt_bias = dt0 + jnp.log(-jnp.expm1(-dt0))
    a_log = jnp.log(jax.random.uniform(ks[7], (DEPTH, SSD_HEADS), f32, 1.0, 16.0))
    d_skip = 1.0 + 0.02 * jax.random.normal(ks[8], (DEPTH, SSD_HEADS), f32)
    ssd_norm_w = 1.0 + 0.02 * jax.random.normal(ks[9], (DEPTH, SSD_INNER), f32)
    w_attn_br = jax.random.normal(ks[10], (DEPTH, ATTN_OUT_WIDTH, D_MODEL), f32) * ATTN_OUT_WIDTH ** -0.5
    w_ssd_br = jax.random.normal(ks[11], (DEPTH, SSD_INNER, D_MODEL), f32) * SSD_INNER ** -0.5
    gate_b = 0.02 * jax.random.normal(ks[12], (DEPTH, N_BRANCHES, D_MODEL), f32)
    w_out = jax.random.normal(ks[13], (DEPTH, D_MODEL, D_MODEL), f32) * D_MODEL ** -0.5
    final_norm_w = 1.0 + 0.02 * jax.random.normal(ks[14], (D_MODEL,), f32)
    return {"x": x, "positions": positions, "norm_w": norm_w, "w_in": w_in, "conv_w": conv_w,
            "conv_b": conv_b, "dt_bias": dt_bias, "a_log": a_log, "d_skip": d_skip,
            "ssd_norm_w": ssd_norm_w, "w_attn_br": w_attn_br, "w_ssd_br": w_ssd_br,
            "gate_b": gate_b, "w_out": w_out, "final_norm_w": final_norm_w}


def reference(x, positions, norm_w, w_in, conv_w, conv_b, dt_bias, a_log, d_skip, ssd_norm_w,
              w_attn_br, w_ssd_br, gate_b, w_out, final_norm_w):
    b, s, _ = x.shape
    split_idx = [int(v) for v in np.cumsum(SPLIT_SIZES)[:-1]]
    h = x
    for layer in range(DEPTH):
        xn = rms_norm(h, norm_w[layer])
        proj = xn @ w_in[layer]
        q, k, v, z_attn, z_ssd, xbc, dt_raw, gate_logits = jnp.split(proj, split_idx, axis=-1)
        grp = (b, s, N_ATTN_GROUPS, ATTN_HEADS_PER_GROUP, ATTN_HEAD_DIM)
        q = partial_rope(q.reshape(grp), positions)
        k = partial_rope(k.reshape(grp), positions)
        v = v.reshape(grp)
        attn = dilated_mixture_attention(q, k, v)
        y_attn = (attn * jax.nn.silu(z_attn.astype(jnp.float32))).astype(x.dtype) @ w_attn_br[layer]
        ssd = ssd_mixer(xbc, z_ssd, dt_raw, conv_w[layer], conv_b[layer], dt_bias[layer],
                        a_log[layer], d_skip[layer], ssd_norm_w[layer])
        y_ssd = ssd.astype(x.dtype) @ w_ssd_br[layer]
        gates = jax.nn.sigmoid(gate_logits.reshape(b, s, N_BRANCHES, D_MODEL) + gate_b[layer])
        merged = gates[:, :, 0] * y_attn + gates[:, :, 1] * y_ssd
        h = h + merged @ w_out[layer]
    return rms_norm(h, final_norm_w)
```

```python
import functools
import math

import jax
import jax.numpy as jnp
from jax import lax
from jax.experimental import pallas as pl
from jax.experimental.pallas import tpu as pltpu

F32 = jnp.float32
BF16 = jnp.bfloat16
HIGHEST = lax.Precision.HIGHEST

D_MODEL = 2048
BATCH = 4
SEQ = 2048
TOKENS = BATCH * SEQ
NORM_EPS = 1e-5

HEAD_DIM = 128
HEADS_PER_GROUP = 8
GROUP_WIDTH = HEADS_PER_GROUP * HEAD_DIM
ATTN_BLOCK = 128
ROPE_DIMS = HEAD_DIM // 4
ROPE_HALF = ROPE_DIMS // 2
ROPE_THETA = 500000.0
Q_SCALE = HEAD_DIM ** -0.5

SSD_INNER = 2 * D_MODEL
SSD_HEAD_DIM = 64
SSD_HEADS = SSD_INNER // SSD_HEAD_DIM
SSD_GROUPS = 8
SSD_HEADS_PER_GROUP = SSD_HEADS // SSD_GROUPS
SSD_STATE = 128
SSD_CONV = 4
SSD_CHUNK = 128
SSD_GROUP_WIDTH = SSD_HEADS_PER_GROUP * SSD_HEAD_DIM
SSD_BC_WIDTH = SSD_GROUPS * SSD_STATE
N_CHUNKS = SEQ // SSD_CHUNK

COL_DT = 3 * 3 * GROUP_WIDTH + GROUP_WIDTH + SSD_INNER + SSD_INNER + 2 * SSD_BC_WIDTH
COL_GATES = COL_DT + SSD_HEADS

LANES = 128
ROW_TILE = 512
COL_TILE = 1024
VMEM_LIMIT = 56 * 1024 * 1024


def _params(*sem):
    return pltpu.CompilerParams(dimension_semantics=sem, vmem_limit_bytes=VMEM_LIMIT)


def _rmsnorm_kernel(x_ref, w_ref, o_ref):
    x = x_ref[...]
    ms = jnp.mean(x * x, axis=-1, keepdims=True)
    o_ref[...] = (x * lax.rsqrt(ms + NORM_EPS) * w_ref[...]).astype(o_ref.dtype)


def _rmsnorm(x2, w):
    return pl.pallas_call(
        _rmsnorm_kernel,
        grid=(TOKENS // ROW_TILE,),
        in_specs=[pl.BlockSpec((ROW_TILE, D_MODEL), lambda i: (i, 0)),
                  pl.BlockSpec((1, D_MODEL), lambda i: (0, 0))],
        out_specs=pl.BlockSpec((ROW_TILE, D_MODEL), lambda i: (i, 0)),
        out_shape=jax.ShapeDtypeStruct((TOKENS, D_MODEL), BF16),
        compiler_params=_params("arbitrary"),
        name="rmsnorm",
    )(x2, w.reshape(1, D_MODEL))


def _rope_table_kernel(pos_ref, invf_ref, cos_ref, sin_ref):
    ang = pos_ref[...].astype(F32) * invf_ref[...]
    lane = lax.broadcasted_iota(jnp.int32, ang.shape, 1)
    sign = jnp.where(lane < ROPE_HALF, -1.0, jnp.where(lane < ROPE_DIMS, 1.0, 0.0))
    cos_ref[...] = jnp.cos(ang)
    sin_ref[...] = jnp.sin(ang) * sign


def _rope_tables(positions):
    inv_freq = ROPE_THETA ** (-jnp.arange(ROPE_HALF, dtype=F32) / ROPE_HALF)
    invf = jnp.zeros((1, LANES), F32).at[0, :ROPE_DIMS].set(jnp.tile(inv_freq, 2))
    rows = 1024
    return pl.pallas_call(
        _rope_table_kernel,
        grid=(TOKENS // rows,),
        in_specs=[pl.BlockSpec((rows, 1), lambda i: (i, 0)),
                  pl.BlockSpec((1, LANES), lambda i: (0, 0))],
        out_specs=[pl.BlockSpec((rows, LANES), lambda i: (i, 0))] * 2,
        out_shape=[jax.ShapeDtypeStruct((TOKENS, LANES), F32)] * 2,
        compiler_params=_params("arbitrary"),
        name="rope_tables",
    )(positions.reshape(TOKENS, 1), invf)


def _inproj_kernel(x_ref, w_ref, cos_ref, sin_ref, o_ref, wb_ref, *, lane_chunks):
    n = pl.program_id(0)

    @pl.when(pl.program_id(1) == 0)
    def _():
        wb_ref[...] = w_ref[...].astype(BF16)

    if lane_chunks == 1:
        x = x_ref[...]
        cos = cos_ref[...]
        sin = sin_ref[...]
    else:
        x = jnp.concatenate(
            [x_ref[:, r * D_MODEL:(r + 1) * D_MODEL] for r in range(lane_chunks)], axis=0)
        cos = jnp.concatenate(
            [cos_ref[:, r * LANES:(r + 1) * LANES] for r in range(lane_chunks)], axis=0)
        sin = jnp.concatenate(
            [sin_ref[:, r * LANES:(r + 1) * LANES] for r in range(lane_chunks)], axis=0)
    acc = jnp.dot(x, wb_ref[...], preferred_element_type=F32)

    @pl.when(n >= 2)
    def _():
        o_ref[...] = acc.astype(o_ref.dtype)

    @pl.when(n < 2)
    def _():
        scale = jnp.where(n == 0, Q_SCALE, 1.0).astype(F32)
        lane = lax.broadcasted_iota(jnp.int32, (ROW_TILE, LANES), 1)
        for h in range(HEADS_PER_GROUP):
            t = acc[:, h * HEAD_DIM:(h + 1) * HEAD_DIM]
            partner = jnp.where(lane < ROPE_HALF,
                                pltpu.roll(t, LANES - ROPE_HALF, 1),
                                pltpu.roll(t, ROPE_HALF, 1))
            o_ref[:, h * HEAD_DIM:(h + 1) * HEAD_DIM] = (
                (t * cos + partner * sin) * scale).astype(o_ref.dtype)


def _inproj(xn, w, cos, sin, variant):
    if variant == 0:
        n_cols = 14
        col_map = lambda n: jnp.where(n < 3, 3 * n, n + 6)
        x_arr, cos_arr, sin_arr = xn, cos, sin
        x_spec = pl.BlockSpec((ROW_TILE, D_MODEL), lambda n, i: (i, 0))
        t_spec = pl.BlockSpec((ROW_TILE, LANES), lambda n, i: (i, 0))
        lane_chunks = 1
    elif variant == 4:
        n_cols = 3
        col_map = lambda n: 3 * n + 1
        x_arr = xn.reshape(BATCH, SEQ // 4, 4 * D_MODEL)
        cos_arr = cos.reshape(BATCH, SEQ // 4, 4 * LANES)
        sin_arr = sin.reshape(BATCH, SEQ // 4, 4 * LANES)
        x_spec = pl.BlockSpec((None, ROW_TILE, D_MODEL), lambda n, i: (i // 4, 0, i % 4))
        t_spec = pl.BlockSpec((None, ROW_TILE, LANES), lambda n, i: (i // 4, 0, i % 4))
        lane_chunks = 1
    else:
        n_cols = 3
        col_map = lambda n: 3 * n + 2
        x_arr = xn.reshape(BATCH, SEQ // 16, 16 * D_MODEL)
        cos_arr = cos.reshape(BATCH, SEQ // 16, 16 * LANES)
        sin_arr = sin.reshape(BATCH, SEQ // 16, 16 * LANES)
        x_spec = pl.BlockSpec((None, SEQ // 16, 4 * D_MODEL), lambda n, i: (i // 4, 0, i % 4))
        t_spec = pl.BlockSpec((None, SEQ // 16, 4 * LANES), lambda n, i: (i // 4, 0, i % 4))
        lane_chunks = 4
    return pl.pallas_call(
        functools.partial(_inproj_kernel, lane_chunks=lane_chunks),
        grid=(n_cols, TOKENS // ROW_TILE),
        in_specs=[x_spec,
                  pl.BlockSpec((D_MODEL, COL_TILE), lambda n, i: (0, col_map(n))),
                  t_spec, t_spec],
        out_specs=pl.BlockSpec((ROW_TILE, COL_TILE), lambda n, i: (i, n)),
        out_shape=jax.ShapeDtypeStruct((TOKENS, n_cols * COL_TILE), BF16),
        scratch_shapes=[pltpu.VMEM((D_MODEL, COL_TILE), BF16)],
        compiler_params=_params("arbitrary", "arbitrary"),
        name=f"inproj_{variant}",
    )(x_arr, w, cos_arr, sin_arr)


def _gate_kernel(x_ref, w_ref, b_ref, o_ref):
    acc = jnp.dot(x_ref[...], w_ref[...], preferred_element_type=F32)
    o_ref[...] = jax.nn.sigmoid(acc + b_ref[...]).astype(o_ref.dtype)


def _gate_proj(xn, w_gate, gate_bias):
    n_out = w_gate.shape[1]
    return pl.pallas_call(
        _gate_kernel,
        grid=(n_out // COL_TILE, TOKENS // ROW_TILE),
        in_specs=[pl.BlockSpec((ROW_TILE, D_MODEL), lambda n, i: (i, 0)),
                  pl.BlockSpec((D_MODEL, COL_TILE), lambda n, i: (0, n)),
                  pl.BlockSpec((1, COL_TILE), lambda n, i: (0, n))],
        out_specs=pl.BlockSpec((ROW_TILE, COL_TILE), lambda n, i: (i, n)),
        out_shape=jax.ShapeDtypeStruct((TOKENS, n_out), BF16),
        compiler_params=_params("arbitrary", "arbitrary"),
        name="gate_proj",
    )(xn, w_gate, gate_bias)


def _dt_kernel(x_ref, w_ref, b_ref, o_ref):
    acc = jnp.dot(x_ref[...], w_ref[...], preferred_element_type=F32)
    o_ref[...] = jax.nn.softplus(acc + b_ref[...])


def _dt_proj(xn, w_dt, dt_bias):
    return pl.pallas_call(
        _dt_kernel,
        grid=(TOKENS // ROW_TILE,),
        in_specs=[pl.BlockSpec((ROW_TILE, D_MODEL), lambda i: (i, 0)),
                  pl.BlockSpec((D_MODEL, LANES), lambda i: (0, 0)),
                  pl.BlockSpec((1, LANES), lambda i: (0, 0))],
        out_specs=pl.BlockSpec((ROW_TILE, LANES), lambda i: (i, 0)),
        out_shape=jax.ShapeDtypeStruct((TOKENS, LANES), F32),
        compiler_params=_params("arbitrary"),
        name="dt_proj",
    )(xn, w_dt, dt_bias)


def _attn_kernel(q0, k0, v0, q1, k1, v1, q2, k2, v2, z_ref, o_ref, m_s, a_s, d_s):
    blk = ATTN_BLOCK
    qi = lax.broadcasted_iota(jnp.int32, (blk, blk), 0)
    ki = lax.broadcasted_iota(jnp.int32, (blk, blk), 1)
    mask_cur = ki <= qi
    mask_prev = ki >= qi
    mask_cat = jnp.concatenate([mask_prev, mask_cur], axis=1)
    nt = (((1,), (1,)), ((), ()))

    def block(q_ref, k_ref, v_ref, lo, has_prev):
        q = q_ref[lo:lo + blk, :]
        if has_prev:
            k = k_ref[lo - blk:lo + blk, :]
            v = v_ref[lo - blk:lo + blk, :]
            mask = mask_cat
        else:
            k = k_ref[lo:lo + blk, :]
            v = v_ref[lo:lo + blk, :]
            mask = mask_cur
        s = lax.dot_general(q, k, nt, preferred_element_type=F32)
        s = jnp.where(mask, s, -jnp.inf)
        m = jnp.max(s, axis=-1, keepdims=True)
        p = jnp.exp(s - m)
        den = jnp.sum(p, axis=-1, keepdims=True)
        acc = jnp.dot(p.astype(BF16), v, preferred_element_type=F32)
        return m, den, acc

    for i in range(SEQ // blk):
        m, den, acc = block(q0, k0, v0, i * blk, i > 0)
        rows = slice(i * blk, (i + 1) * blk)
        m_s[rows, :] = jnp.broadcast_to(m, (blk, LANES))
        d_s[rows, :] = jnp.broadcast_to(den, (blk, LANES))
        a_s[rows, :] = acc

    def merge(rows, m, den, acc):
        m_old = m_s[rows, :]
        m_new = jnp.maximum(m_old, m)
        w_old = jnp.exp(m_old - m_new)
        w_new = jnp.exp(m - m_new)
        a_s[rows, :] = w_old * a_s[rows, :] + w_new * acc
        d_s[rows, :] = w_old * d_s[rows, :] + w_new * den
        m_s[rows, :] = m_new

    for q_ref, k_ref, v_ref, dil in ((q1, k1, v1, 4), (q2, k2, v2, 16)):
        sub = SEQ // dil
        for r in range(dil):
            for j in range(sub // blk):
                m, den, acc = block(q_ref, k_ref, v_ref, r * sub + j * blk, j > 0)
                merge(pl.ds(j * blk * dil + r, blk, stride=dil), m, den, acc)

    z = z_ref[...].astype(F32)
    out = a_s[...] / d_s[...]
    o_ref[...] = (out * (z * jax.nn.sigmoid(z))).astype(o_ref.dtype)


def _attention(p_nat, p4, p16):
    hb = HEADS_PER_GROUP

    def spec(sec):
        return pl.BlockSpec((SEQ, HEAD_DIM), lambda b, h: (b, sec * hb + h))

    qkv = [spec(0), spec(1), spec(2)]
    return pl.pallas_call(
        _attn_kernel,
        grid=(BATCH, hb),
        in_specs=qkv * 3 + [spec(3)],
        out_specs=pl.BlockSpec((SEQ, HEAD_DIM), lambda b, h: (b, h)),
        out_shape=jax.ShapeDtypeStruct((TOKENS, GROUP_WIDTH), BF16),
        scratch_shapes=[pltpu.VMEM((SEQ, LANES), F32)] * 3,
        compiler_params=_params("arbitrary", "arbitrary"),
        name="dilated_attn",
    )(p_nat, p_nat, p_nat, p4, p4, p4, p16, p16, p16, p_nat)


def _ssd_kernel(xs_ref, b_ref, c_ref, z_ref, dt_ref, dtt_ref, cw_ref, cb_ref, arow_ref, acol_ref,
                dskip_ref, nw_ref, o_ref,
                tail_x, tail_b, tail_c, state, y_s, dt3, cs3, cst_s):
    L = SSD_CHUNK
    G = SSD_GROUPS
    GW = SSD_GROUP_WIDTH
    N = SSD_STATE

    @pl.when(pl.program_id(1) == 0)
    def _():
        tail_x[...] = jnp.zeros_like(tail_x)
        tail_b[...] = jnp.zeros_like(tail_b)
        tail_c[...] = jnp.zeros_like(tail_c)
        state[...] = jnp.zeros_like(state)

    li = lax.broadcasted_iota(jnp.int32, (L, L), 0)
    si = lax.broadcasted_iota(jnp.int32, (L, L), 1)
    tri = li >= si
    tri_f = tri.astype(F32)
    a_row = -jnp.exp(arow_ref[...])
    a_col = -jnp.exp(acol_ref[...])
    dt = dt_ref[...]
    cs = jnp.dot(tri_f, dt * a_row, precision=HIGHEST, preferred_element_type=F32)
    cst = jnp.dot(dtt_ref[...] * a_col, (si >= li).astype(F32), precision=HIGHEST,
                  preferred_element_type=F32)
    cst_s[...] = cst
    for g in range(G):
        dt3[g] = dt[:, g * SSD_HEADS_PER_GROUP:(g + 1) * SSD_HEADS_PER_GROUP]
        cs3[g] = cs[:, g * SSD_HEADS_PER_GROUP:(g + 1) * SSD_HEADS_PER_GROUP]

    row8 = lax.broadcasted_iota(jnp.int32, (8, 1), 0)

    def conv_silu(x_ref, tail_ref, col, width, wcol):
        cols = pl.ds(col, width)
        wcols = pl.ds(wcol, width)
        x = x_ref[:, cols].astype(F32)
        prev = tail_ref[:, cols]
        w = cw_ref[:, wcols]
        acc = x * w[SSD_CONV - 1:SSD_CONV, :] + cb_ref[:, wcols]
        for s in range(1, SSD_CONV):
            xr = pltpu.roll(x, s, 0)
            pr = pltpu.roll(prev, s, 0)
            head = jnp.where(row8 < s, pr, xr[0:8, :])
            shifted = jnp.concatenate([head, xr[8:, :]], axis=0)
            acc = acc + shifted * w[SSD_CONV - 1 - s:SSD_CONV - s, :]
        tail_ref[:, cols] = x[L - 8:L, :]
        return acc * jax.nn.sigmoid(acc)

    lane = lax.broadcasted_iota(jnp.int32, (L, LANES), 1)
    lo_half = lane < SSD_HEAD_DIM
    nt = (((1,), (1,)), ((), ()))

    def group(g, carry):
        xcol = pl.multiple_of(g * GW, GW)
        ncol = pl.multiple_of(g * N, N)
        x_g = conv_silu(xs_ref, tail_x, xcol, GW, xcol)
        b_g = conv_silu(b_ref, tail_b, ncol, N, pl.multiple_of(SSD_INNER + ncol, N))
        c_g = conv_silu(c_ref, tail_c, ncol, N, pl.multiple_of(SSD_INNER + SSD_BC_WIDTH + ncol, N))
        b_bf = b_g.astype(BF16)
        c_bf = c_g.astype(BF16)
        dt_g = dt3[g]
        cs_g = cs3[g]
        cst_g = cst_s[pl.ds(pl.multiple_of(g * SSD_HEADS_PER_GROUP, 8), SSD_HEADS_PER_GROUP), :]
        ecs_g = jnp.exp(cs_g)
        dte_g = jnp.exp(cs_g[L - 1:L, :] - cs_g)
        cb = lax.dot_general(c_bf, b_bf, nt, preferred_element_type=F32)
        st_prev = state[g]
        y_off = jnp.dot(c_bf, st_prev.astype(BF16), preferred_element_type=F32)
        y_parts = []
        xw_parts = []
        ecx_parts = []
        for p in range(SSD_HEADS_PER_GROUP // 2):
            ja, jb = 2 * p, 2 * p + 1
            x_p = x_g[:, p * LANES:(p + 1) * LANES]
            dtx = jnp.where(lo_half, dt_g[:, ja:ja + 1], dt_g[:, jb:jb + 1])
            ecx = jnp.where(lo_half, ecs_g[:, ja:ja + 1], ecs_g[:, jb:jb + 1])
            dtex = jnp.where(lo_half, dte_g[:, ja:ja + 1], dte_g[:, jb:jb + 1])
            xdt = x_p * dtx
            gmats = []
            for j in (ja, jb):
                diff = cs_g[:, j:j + 1] - cst_g[j:j + 1, :]
                decay = jnp.exp(jnp.where(tri, diff, -jnp.inf))
                gmats.append((cb * decay).astype(BF16))
            lhs = jnp.concatenate(gmats, axis=1)
            rhs = jnp.concatenate([jnp.where(lo_half, xdt, 0.0), jnp.where(lo_half, 0.0, xdt)],
                                  axis=0).astype(BF16)
            y_diag = jnp.dot(lhs, rhs, preferred_element_type=F32)
            y_parts.append(y_diag + y_off[:, p * LANES:(p + 1) * LANES] * ecx)
            xw_parts.append((xdt * dtex).astype(BF16))
            ecx_parts.append(ecx[L - 1:L, :])
        y = jnp.concatenate(y_parts, axis=1)
        xw = jnp.concatenate(xw_parts, axis=1)
        chunk_decay = jnp.concatenate(ecx_parts, axis=1)
        st_chunk = jnp.dot(b_g.T.astype(BF16), xw, preferred_element_type=F32)
        state[g] = st_prev * chunk_decay + st_chunk
        cols = pl.ds(xcol, GW)
        y = y + x_g * dskip_ref[:, cols]
        z = z_ref[:, cols].astype(F32)
        y_s[:, cols] = y * (z * jax.nn.sigmoid(z))
        return carry

    lax.fori_loop(0, G, group, 0)

    y = y_s[...]
    ms = jnp.mean(y * y, axis=-1, keepdims=True)
    o_ref[...] = (y * lax.rsqrt(ms + NORM_EPS) * nw_ref[...]).astype(o_ref.dtype)


def _ssd(p_nat, dt, dt_t, conv_w, conv_b, a_log, d_skip, norm_w):
    L = SSD_CHUNK
    row = lambda b, c: b * N_CHUNKS + c
    const = lambda b, c: (0, 0)
    a_pad = jnp.zeros((LANES,), F32).at[:SSD_HEADS].set(a_log)
    return pl.pallas_call(
        _ssd_kernel,
        grid=(BATCH, N_CHUNKS),
        in_specs=[
            pl.BlockSpec((L, SSD_INNER), lambda b, c: (row(b, c), 2)),
            pl.BlockSpec((L, SSD_BC_WIDTH), lambda b, c: (row(b, c), 12)),
            pl.BlockSpec((L, SSD_BC_WIDTH), lambda b, c: (row(b, c), 13)),
            pl.BlockSpec((L, SSD_INNER), lambda b, c: (row(b, c), 1)),
            pl.BlockSpec((L, LANES), lambda b, c: (row(b, c), 0)),
            pl.BlockSpec((LANES, L), lambda b, c: (0, row(b, c))),
            pl.BlockSpec((SSD_CONV, SSD_INNER + 2 * SSD_BC_WIDTH), const),
            pl.BlockSpec((1, SSD_INNER + 2 * SSD_BC_WIDTH), const),
            pl.BlockSpec((1, LANES), const),
            pl.BlockSpec((LANES, 1), const),
            pl.BlockSpec((1, SSD_INNER), const),
            pl.BlockSpec((1, SSD_INNER), const),
        ],
        out_specs=pl.BlockSpec((L, SSD_INNER), lambda b, c: (row(b, c), 0)),
        out_shape=jax.ShapeDtypeStruct((TOKENS, SSD_INNER), BF16),
        scratch_shapes=[
            pltpu.VMEM((8, SSD_INNER), F32),
            pltpu.VMEM((8, SSD_BC_WIDTH), F32),
            pltpu.VMEM((8, SSD_BC_WIDTH), F32),
            pltpu.VMEM((SSD_GROUPS, SSD_STATE, SSD_GROUP_WIDTH), F32),
            pltpu.VMEM((L, SSD_INNER), F32),
            pltpu.VMEM((SSD_GROUPS, L, SSD_HEADS_PER_GROUP), F32),
            pltpu.VMEM((SSD_GROUPS, L, SSD_HEADS_PER_GROUP), F32),
            pltpu.VMEM((LANES, L), F32),
        ],
        compiler_params=_params("arbitrary", "arbitrary"),
        name="ssd",
    )(p_nat, p_nat, p_nat, p_nat, dt, dt_t,
      conv_w, conv_b.reshape(1, -1), a_pad.reshape(1, LANES), a_pad.reshape(LANES, 1),
      jnp.repeat(d_skip, SSD_HEAD_DIM).reshape(1, SSD_INNER), norm_w.reshape(1, SSD_INNER))


def _branch_kernel(a_ref, s_ref, wa_ref, ws_ref, g0_ref, g1_ref, o_ref, wab, wsb):
    @pl.when(pl.program_id(1) == 0)
    def _():
        wab[...] = wa_ref[...].astype(BF16)
        wsb[...] = ws_ref[...].astype(BF16)

    ya = jnp.dot(a_ref[...], wab[...], preferred_element_type=F32)
    ys = jnp.dot(s_ref[...], wsb[...], preferred_element_type=F32)
    merged = g0_ref[...].astype(F32) * ya + g1_ref[...].astype(F32) * ys
    o_ref[...] = merged.astype(o_ref.dtype)


def _branches(attn, ssd, w_attn, w_ssd, gates):
    tn = 512
    nj = D_MODEL // tn
    return pl.pallas_call(
        _branch_kernel,
        grid=(nj, TOKENS // ROW_TILE),
        in_specs=[pl.BlockSpec((ROW_TILE, GROUP_WIDTH), lambda n, i: (i, 0)),
                  pl.BlockSpec((ROW_TILE, SSD_INNER), lambda n, i: (i, 0)),
                  pl.BlockSpec((GROUP_WIDTH, tn), lambda n, i: (0, n)),
                  pl.BlockSpec((SSD_INNER, tn), lambda n, i: (0, n)),
                  pl.BlockSpec((ROW_TILE, tn), lambda n, i: (i, n)),
                  pl.BlockSpec((ROW_TILE, tn), lambda n, i: (i, nj + n))],
        out_specs=pl.BlockSpec((ROW_TILE, tn), lambda n, i: (i, n)),
        out_shape=jax.ShapeDtypeStruct((TOKENS, D_MODEL), BF16),
        scratch_shapes=[pltpu.VMEM((GROUP_WIDTH, tn), BF16), pltpu.VMEM((SSD_INNER, tn), BF16)],
        compiler_params=_params("arbitrary", "arbitrary"),
        name="branches",
    )(attn, ssd, w_attn, w_ssd, gates, gates)


def _out_kernel(m_ref, w_ref, x_ref, fw_ref, o_ref):
    h = x_ref[...] + jnp.dot(m_ref[...], w_ref[...], preferred_element_type=F32)
    ms = jnp.mean(h * h, axis=-1, keepdims=True)
    o_ref[...] = h * lax.rsqrt(ms + NORM_EPS) * fw_ref[...]


def _out_proj(merged, w_out_bf, x2, final_w):
    tm = 256
    return pl.pallas_call(
        _out_kernel,
        grid=(TOKENS // tm,),
        in_specs=[pl.BlockSpec((tm, D_MODEL), lambda i: (i, 0)),
                  pl.BlockSpec((D_MODEL, D_MODEL), lambda i: (0, 0)),
                  pl.BlockSpec((tm, D_MODEL), lambda i: (i, 0)),
                  pl.BlockSpec((1, D_MODEL), lambda i: (0, 0))],
        out_specs=pl.BlockSpec((tm, D_MODEL), lambda i: (i, 0)),
        out_shape=jax.ShapeDtypeStruct((TOKENS, D_MODEL), F32),
        compiler_params=_params("arbitrary"),
        name="out_proj",
    )(merged, w_out_bf, x2, final_w.reshape(1, D_MODEL))


def kernel(x, positions, norm_w, w_in, conv_w, conv_b, dt_bias, a_log, d_skip, ssd_norm_w,
           w_attn_br, w_ssd_br, gate_b, w_out, final_norm_w):
    assert x.shape == (BATCH, SEQ, D_MODEL) and w_in.shape[0] == 1
    x2 = x.reshape(TOKENS, D_MODEL)
    w = w_in[0]
    xn = _rmsnorm(x2, norm_w[0])
    cos, sin = _rope_tables(positions)
    p_nat = _inproj(xn, w, cos, sin, 0)
    p4 = _inproj(xn, w, cos, sin, 4)
    p16 = _inproj(xn, w, cos, sin, 16)
    gates = _gate_proj(xn, w[:, COL_GATES:].astype(BF16), gate_b[0].reshape(1, 2 * D_MODEL))
    w_dt = jnp.zeros((D_MODEL, LANES), BF16).at[:, :SSD_HEADS].set(w[:, COL_DT:COL_GATES].astype(BF16))
    dt_b = jnp.zeros((1, LANES), F32).at[0, :SSD_HEADS].set(dt_bias[0])
    dt = _dt_proj(xn, w_dt, dt_b)
    attn = _attention(p_nat, p4, p16)
    ssd = _ssd(p_nat, dt, dt.T, conv_w[0], conv_b[0], a_log[0], d_skip[0], ssd_norm_w[0])
    merged = _branches(attn, ssd, w_attn_br[0], w_ssd_br[0], gates)
    out = _out_proj(merged, w_out[0].astype(BF16), x2, final_norm_w)
    return out.reshape(BATCH, SEQ, D_MODEL)
```

```python
import functools

import jax
import jax.numpy as jnp
from jax import lax
from jax.experimental import pallas as pl
from jax.experimental.pallas import tpu as pltpu

F32 = jnp.float32
BF16 = jnp.bfloat16
HIGHEST = lax.Precision.HIGHEST

D_MODEL = 2048
BATCH = 4
SEQ = 2048
TOKENS = BATCH * SEQ
NORM_EPS = 1e-5

HEAD_DIM = 128
HEADS_PER_GROUP = 8
GROUP_WIDTH = HEADS_PER_GROUP * HEAD_DIM
ATTN_BLOCK = 128
DILATIONS = (4, 16)
ROPE_DIMS = HEAD_DIM // 4
ROPE_HALF = ROPE_DIMS // 2
ROPE_THETA = 500000.0
Q_SCALE = HEAD_DIM ** -0.5

SSD_INNER = 2 * D_MODEL
SSD_HEAD_DIM = 64
SSD_HEADS = SSD_INNER // SSD_HEAD_DIM
SSD_GROUPS = 8
SSD_HEADS_PER_GROUP = SSD_HEADS // SSD_GROUPS
SSD_STATE = 128
SSD_CONV = 4
SSD_CHUNK = 128
SSD_GROUP_WIDTH = SSD_HEADS_PER_GROUP * SSD_HEAD_DIM
SSD_BC_WIDTH = SSD_GROUPS * SSD_STATE
SSD_CONV_CH = SSD_INNER + 2 * SSD_BC_WIDTH
N_CHUNKS = SEQ // SSD_CHUNK

COL_DT = 3 * 3 * GROUP_WIDTH + GROUP_WIDTH + SSD_INNER + SSD_CONV_CH
COL_GATES = COL_DT + SSD_HEADS

LANES = 128
ROW_TILE = 512
COL_TILE = 1024
VMEM_LIMIT = 56 * 1024 * 1024


def _params(*sem):
    return pltpu.CompilerParams(dimension_semantics=sem, vmem_limit_bytes=VMEM_LIMIT)


def _rmsnorm_kernel(x_ref, w_ref, o_ref):
    x = x_ref[...]
    ms = jnp.mean(x * x, axis=-1, keepdims=True)
    o_ref[...] = (x * lax.rsqrt(ms + NORM_EPS) * w_ref[...]).astype(o_ref.dtype)


def _rmsnorm(x2, w):
    return pl.pallas_call(
        _rmsnorm_kernel,
        grid=(TOKENS // ROW_TILE,),
        in_specs=[pl.BlockSpec((ROW_TILE, D_MODEL), lambda i: (i, 0)),
                  pl.BlockSpec((1, D_MODEL), lambda i: (0, 0))],
        out_specs=pl.BlockSpec((ROW_TILE, D_MODEL), lambda i: (i, 0)),
        out_shape=jax.ShapeDtypeStruct((TOKENS, D_MODEL), BF16),
        compiler_params=_params("arbitrary"),
        name="rmsnorm",
    )(x2, w.reshape(1, D_MODEL))


def _rope_table_kernel(pos_ref, invf_ref, cos_ref, sin_ref):
    ang = pos_ref[...].astype(F32) * invf_ref[...]
    lane = lax.broadcasted_iota(jnp.int32, ang.shape, 1)
    sign = jnp.where(lane < ROPE_HALF, -1.0, jnp.where(lane < ROPE_DIMS, 1.0, 0.0))
    cos_ref[...] = jnp.cos(ang)
    sin_ref[...] = jnp.sin(ang) * sign


def _rope_tables(positions):
    inv_freq = ROPE_THETA ** (-jnp.arange(ROPE_HALF, dtype=F32) / ROPE_HALF)
    invf = jnp.zeros((1, LANES), F32).at[0, :ROPE_DIMS].set(jnp.tile(inv_freq, 2))
    rows = 1024
    return pl.pallas_call(
        _rope_table_kernel,
        grid=(TOKENS // rows,),
        in_specs=[pl.BlockSpec((rows, 1), lambda i: (i, 0)),
                  pl.BlockSpec((1, LANES), lambda i: (0, 0))],
        out_specs=[pl.BlockSpec((rows, LANES), lambda i: (i, 0))] * 2,
        out_shape=[jax.ShapeDtypeStruct((TOKENS, LANES), F32)] * 2,
        compiler_params=_params("arbitrary"),
        name="rope_tables",
    )(positions.reshape(TOKENS, 1), invf)


def _inproj_kernel(x_ref, w_ref, cos_ref, sin_ref, o_ref, wb_ref, *, n_q, n_k):
    n = pl.program_id(0)

    @pl.when(pl.program_id(1) == 0)
    def _():
        wb_ref[...] = w_ref[...].astype(BF16)

    acc = jnp.dot(x_ref[...], wb_ref[...], preferred_element_type=F32)

    @pl.when(n >= n_q + n_k)
    def _():
        o_ref[...] = acc.astype(o_ref.dtype)

    @pl.when(n < n_q + n_k)
    def _():
        cos = cos_ref[...]
        sin = sin_ref[...]
        scale = jnp.where(n < n_q, Q_SCALE, 1.0).astype(F32)
        lane = lax.broadcasted_iota(jnp.int32, (ROW_TILE, LANES), 1)
        for h in range(HEADS_PER_GROUP):
            t = acc[:, h * HEAD_DIM:(h + 1) * HEAD_DIM]
            partner = jnp.where(lane < ROPE_HALF,
                                pltpu.roll(t, LANES - ROPE_HALF, 1),
                                pltpu.roll(t, ROPE_HALF, 1))
            o_ref[:, h * HEAD_DIM:(h + 1) * HEAD_DIM] = (
                (t * cos + partner * sin) * scale).astype(o_ref.dtype)


def _inproj(xn, w_in, cos, sin, dilated):
    if dilated:
        n_cols, n_q, n_k, dtype = 6, 2, 2, F32
        col_map = lambda n: n + 1 + n // 2
    else:
        n_cols, n_q, n_k, dtype = 14, 1, 1, BF16
        col_map = lambda n: jnp.where(n < 3, 3 * n, n + 6)
    t_spec = pl.BlockSpec((ROW_TILE, LANES), lambda n, i: (i, 0))
    return pl.pallas_call(
        functools.partial(_inproj_kernel, n_q=n_q, n_k=n_k),
        grid=(n_cols, TOKENS // ROW_TILE),
        in_specs=[pl.BlockSpec((ROW_TILE, D_MODEL), lambda n, i: (i, 0)),
                  pl.BlockSpec((None, D_MODEL, COL_TILE), lambda n, i: (0, 0, col_map(n))),
                  t_spec, t_spec],
        out_specs=pl.BlockSpec((ROW_TILE, COL_TILE), lambda n, i: (i, n)),
        out_shape=jax.ShapeDtypeStruct((TOKENS, n_cols * COL_TILE), dtype),
        scratch_shapes=[pltpu.VMEM((D_MODEL, COL_TILE), BF16)],
        compiler_params=_params("arbitrary", "arbitrary"),
        name="inproj_dilated" if dilated else "inproj_main",
    )(xn, w_in, cos, sin)


def _gate_kernel(x_ref, wa_ref, wb_ref, b_ref, o_ref, w_s):
    @pl.when(pl.program_id(1) == 0)
    def _():
        half = LANES // 2
        lane = lax.broadcasted_iota(jnp.int32, (D_MODEL, LANES), 1)
        n_chunks = COL_TILE // LANES
        rolled = [pltpu.roll(wa_ref[:, c * LANES:(c + 1) * LANES], half, 1) for c in range(n_chunks)]
        rolled.append(pltpu.roll(wb_ref[...], half, 1))
        for c in range(n_chunks):
            w_s[:, c * LANES:(c + 1) * LANES] = jnp.where(lane < half, rolled[c], rolled[c + 1]).astype(BF16)

    acc = jnp.dot(x_ref[...], w_s[...], preferred_element_type=F32)
    o_ref[...] = jax.nn.sigmoid(acc + b_ref[...]).astype(o_ref.dtype)


def _gate_proj(xn, w_in, gate_bias):
    n_out = 2 * D_MODEL
    first = COL_DT // COL_TILE
    per = COL_TILE // LANES
    return pl.pallas_call(
        _gate_kernel,
        grid=(n_out // COL_TILE, TOKENS // ROW_TILE),
        in_specs=[pl.BlockSpec((ROW_TILE, D_MODEL), lambda n, i: (i, 0)),
                  pl.BlockSpec((None, D_MODEL, COL_TILE), lambda n, i: (0, 0, first + n)),
                  pl.BlockSpec((None, D_MODEL, LANES), lambda n, i: (0, 0, (first + n + 1) * per)),
                  pl.BlockSpec((1, COL_TILE), lambda n, i: (0, n))],
        out_specs=pl.BlockSpec((ROW_TILE, COL_TILE), lambda n, i: (i, n)),
        out_shape=jax.ShapeDtypeStruct((TOKENS, n_out), BF16),
        scratch_shapes=[pltpu.VMEM((D_MODEL, COL_TILE), BF16)],
        compiler_params=_params("arbitrary", "arbitrary"),
        name="gate_proj",
    )(xn, w_in, w_in, gate_bias)


def _dt_kernel(x_ref, w_ref, b_ref, o_ref):
    acc = jnp.dot(x_ref[...], w_ref[...].astype(BF16), preferred_element_type=F32)
    o_ref[...] = jax.nn.softplus(acc + b_ref[...])


def _dt_proj(xn, w_in, dt_bias):
    return pl.pallas_call(
        _dt_kernel,
        grid=(TOKENS // ROW_TILE,),
        in_specs=[pl.BlockSpec((ROW_TILE, D_MODEL), lambda i: (i, 0)),
                  pl.BlockSpec((None, D_MODEL, LANES), lambda i: (0, 0, COL_DT // LANES)),
                  pl.BlockSpec((1, LANES), lambda i: (0, 0))],
        out_specs=pl.BlockSpec((ROW_TILE, LANES), lambda i: (i, 0)),
        out_shape=jax.ShapeDtypeStruct((TOKENS, LANES), F32),
        compiler_params=_params("arbitrary"),
        name="dt_proj",
    )(xn, w_in, dt_bias)


def _attn_kernel(q0, k0, v0, q1, k1, v1, q2, k2, v2, z_ref, o_ref, m_s, a_s, d_s, *perm):
    blk = ATTN_BLOCK
    qi = lax.broadcasted_iota(jnp.int32, (blk, blk), 0)
    ki = lax.broadcasted_iota(jnp.int32, (blk, blk), 1)
    mask_cur = ki <= qi
    mask_prev = ki >= qi
    mask_cat = jnp.concatenate([mask_prev, mask_cur], axis=1)
    nt = (((1,), (1,)), ((), ()))

    for src, dst, dil in zip((q1, k1, v1, q2, k2, v2), perm, (DILATIONS[0],) * 3 + (DILATIONS[1],) * 3):
        sub = SEQ // dil
        for r in range(dil):
            dst[r * sub:(r + 1) * sub, :] = src[pl.ds(r, sub, stride=dil), :].astype(BF16)

    def block(q_ref, k_ref, v_ref, lo, has_prev):
        q = q_ref[lo:lo + blk, :]
        if has_prev:
            k = k_ref[lo - blk:lo + blk, :]
            v = v_ref[lo - blk:lo + blk, :]
            mask = mask_cat
        else:
            k = k_ref[lo:lo + blk, :]
            v = v_ref[lo:lo + blk, :]
            mask = mask_cur
        s = lax.dot_general(q, k, nt, preferred_element_type=F32)
        s = jnp.where(mask, s, -jnp.inf)
        m = jnp.max(s, axis=-1, keepdims=True)
        p = jnp.exp(s - m)
        den = jnp.sum(p, axis=-1, keepdims=True)
        acc = jnp.dot(p.astype(BF16), v, preferred_element_type=F32)
        return m, den, acc

    for i in range(SEQ // blk):
        m, den, acc = block(q0, k0, v0, i * blk, i > 0)
        rows = slice(i * blk, (i + 1) * blk)
        m_s[rows, :] = jnp.broadcast_to(m, (blk, LANES))
        d_s[rows, :] = jnp.broadcast_to(den, (blk, LANES))
        a_s[rows, :] = acc

    def merge(rows, m, den, acc):
        m_old = m_s[rows, :]
        m_new = jnp.maximum(m_old, m)
        w_old = jnp.exp(m_old - m_new)
        w_new = jnp.exp(m - m_new)
        a_s[rows, :] = w_old * a_s[rows, :] + w_new * acc
        d_s[rows, :] = w_old * d_s[rows, :] + w_new * den
        m_s[rows, :] = m_new

    for g, dil in enumerate(DILATIONS):
        q_ref, k_ref, v_ref = perm[3 * g:3 * g + 3]
        sub = SEQ // dil
        for r in range(dil):
            for j in range(sub // blk):
                m, den, acc = block(q_ref, k_ref, v_ref, r * sub + j * blk, j > 0)
                merge(pl.ds(j * blk * dil + r, blk, stride=dil), m, den, acc)

    z = z_ref[...].astype(F32)
    out = a_s[...] / d_s[...]
    o_ref[...] = (out * (z * jax.nn.sigmoid(z))).astype(o_ref.dtype)


def _attention(p_main, p_dil):
    hb = HEADS_PER_GROUP

    def spec(sec):
        return pl.BlockSpec((SEQ, HEAD_DIM), lambda b, h: (b, sec * hb + h))

    return pl.pallas_call(
        _attn_kernel,
        grid=(BATCH, hb),
        in_specs=[spec(0), spec(1), spec(2), spec(0), spec(2), spec(4), spec(1), spec(3), spec(5), spec(3)],
        out_specs=pl.BlockSpec((SEQ, HEAD_DIM), lambda b, h: (b, h)),
        out_shape=jax.ShapeDtypeStruct((TOKENS, GROUP_WIDTH), BF16),
        scratch_shapes=[pltpu.VMEM((SEQ, LANES), F32)] * 3 + [pltpu.VMEM((SEQ, HEAD_DIM), BF16)] * 6,
        compiler_params=_params("arbitrary", "arbitrary"),
        name="dilated_attn",
    )(p_main, p_main, p_main, p_dil, p_dil, p_dil, p_dil, p_dil, p_dil, p_main)


def _ssd_kernel(xs_ref, b_ref, c_ref, z_ref, dt_ref, dtt_ref, cw_ref, cb_ref, arow_ref, acol_ref,
                dskip_ref, nw_ref, o_ref,
                tail_x, tail_b, tail_c, state, y_s, dt3, cs3, cst_s):
    L = SSD_CHUNK
    G = SSD_GROUPS
    GW = SSD_GROUP_WIDTH
    N = SSD_STATE

    @pl.when(pl.program_id(1) == 0)
    def _():
        tail_x[...] = jnp.zeros_like(tail_x)
        tail_b[...] = jnp.zeros_like(tail_b)
        tail_c[...] = jnp.zeros_like(tail_c)
        state[...] = jnp.zeros_like(state)

    li = lax.broadcasted_iota(jnp.int32, (L, L), 0)
    si = lax.broadcasted_iota(jnp.int32, (L, L), 1)
    tri = li >= si
    tri_f = tri.astype(F32)
    a_row = -jnp.exp(arow_ref[...])
    a_col = -jnp.exp(acol_ref[...])
    dt = dt_ref[...]
    cs = jnp.dot(tri_f, dt * a_row, precision=HIGHEST, preferred_element_type=F32)
    cst = jnp.dot(dtt_ref[...] * a_col, (si >= li).astype(F32), precision=HIGHEST,
                  preferred_element_type=F32)
    cst_s[...] = cst
    for g in range(G):
        dt3[g] = dt[:, g * SSD_HEADS_PER_GROUP:(g + 1) * SSD_HEADS_PER_GROUP]
        cs3[g] = cs[:, g * SSD_HEADS_PER_GROUP:(g + 1) * SSD_HEADS_PER_GROUP]

    row8 = lax.broadcasted_iota(jnp.int32, (8, 1), 0)

    def conv_silu(x_ref, tail_ref, col, width, wcol):
        cols = pl.ds(col, width)
        wcols = pl.ds(wcol, width)
        x = x_ref[:, cols].astype(F32)
        prev = tail_ref[:, cols]
        w = cw_ref[:, wcols]
        acc = x * w[SSD_CONV - 1:SSD_CONV, :] + cb_ref[:, wcols]
        for s in range(1, SSD_CONV):
            xr = pltpu.roll(x, s, 0)
            pr = pltpu.roll(prev, s, 0)
            head = jnp.where(row8 < s, pr, xr[0:8, :])
            shifted = jnp.concatenate([head, xr[8:, :]], axis=0)
            acc = acc + shifted * w[SSD_CONV - 1 - s:SSD_CONV - s, :]
        tail_ref[:, cols] = x[L - 8:L, :]
        return acc * jax.nn.sigmoid(acc)

    lane = lax.broadcasted_iota(jnp.int32, (L, LANES), 1)
    lo_half = lane < SSD_HEAD_DIM
    nt = (((1,), (1,)), ((), ()))

    def group(g, carry):
        xcol = pl.multiple_of(g * GW, GW)
        ncol = pl.multiple_of(g * N, N)
        x_g = conv_silu(xs_ref, tail_x, xcol, GW, xcol)
        b_g = conv_silu(b_ref, tail_b, ncol, N, pl.multiple_of(SSD_INNER + ncol, N))
        c_g = conv_silu(c_ref, tail_c, ncol, N, pl.multiple_of(SSD_INNER + SSD_BC_WIDTH + ncol, N))
        b_bf = b_g.astype(BF16)
        c_bf = c_g.astype(BF16)
        dt_g = dt3[g]
        cs_g = cs3[g]
        cst_g = cst_s[pl.ds(pl.multiple_of(g * SSD_HEADS_PER_GROUP, 8), SSD_HEADS_PER_GROUP), :]
        ecs_g = jnp.exp(cs_g)
        dte_g = jnp.exp(cs_g[L - 1:L, :] - cs_g)
        cb = lax.dot_general(c_bf, b_bf, nt, preferred_element_type=F32)
        st_prev = state[g]
        y_off = jnp.dot(c_bf, st_prev.astype(BF16), preferred_element_type=F32)
        y_parts = []
        xw_parts = []
        ecx_parts = []
        for p in range(SSD_HEADS_PER_GROUP // 2):
            ja, jb = 2 * p, 2 * p + 1
            x_p = x_g[:, p * LANES:(p + 1) * LANES]
            dtx = jnp.where(lo_half, dt_g[:, ja:ja + 1], dt_g[:, jb:jb + 1])
            ecx = jnp.where(lo_half, ecs_g[:, ja:ja + 1], ecs_g[:, jb:jb + 1])
            dtex = jnp.where(lo_half, dte_g[:, ja:ja + 1], dte_g[:, jb:jb + 1])
            xdt = x_p * dtx
            gmats = []
            for j in (ja, jb):
                diff = cs_g[:, j:j + 1] - cst_g[j:j + 1, :]
                decay = jnp.exp(jnp.where(tri, diff, -jnp.inf))
                gmats.append((cb * decay).astype(BF16))
            lhs = jnp.concatenate(gmats, axis=1)
            rhs = jnp.concatenate([jnp.where(lo_half, xdt, 0.0), jnp.where(lo_half, 0.0, xdt)],
                                  axis=0).astype(BF16)
            y_diag = jnp.dot(lhs, rhs, preferred_element_type=F32)
            y_parts.append(y_diag + y_off[:, p * LANES:(p + 1) * LANES] * ecx)
            xw_parts.append((xdt * dtex).astype(BF16))
            ecx_parts.append(ecx[L - 1:L, :])
        y = jnp.concatenate(y_parts, axis=1)
        xw = jnp.concatenate(xw_parts, axis=1)
        chunk_decay = jnp.concatenate(ecx_parts, axis=1)
        st_chunk = jnp.dot(b_g.T.astype(BF16), xw, preferred_element_type=F32)
        state[g] = st_prev * chunk_decay + st_chunk
        cols = pl.ds(xcol, GW)
        y = y + x_g * dskip_ref[:, cols]
        z = z_ref[:, cols].astype(F32)
        y_s[:, cols] = y * (z * jax.nn.sigmoid(z))
        return carry

    lax.fori_loop(0, G, group, 0)

    y = y_s[...]
    ms = jnp.mean(y * y, axis=-1, keepdims=True)
    o_ref[...] = (y * lax.rsqrt(ms + NORM_EPS) * nw_ref[...]).astype(o_ref.dtype)


def _ssd(p_main, dt, dt_t, conv_w, conv_b, a_log, d_skip, norm_w):
    L = SSD_CHUNK
    row = lambda b, c: b * N_CHUNKS + c
    const = lambda b, c: (0, 0)
    a_pad = jnp.zeros((LANES,), F32).at[:SSD_HEADS].set(a_log)
    return pl.pallas_call(
        _ssd_kernel,
        grid=(BATCH, N_CHUNKS),
        in_specs=[
            pl.BlockSpec((L, SSD_INNER), lambda b, c: (row(b, c), 2)),
            pl.BlockSpec((L, SSD_BC_WIDTH), lambda b, c: (row(b, c), 12)),
            pl.BlockSpec((L, SSD_BC_WIDTH), lambda b, c: (row(b, c), 13)),
            pl.BlockSpec((L, SSD_INNER), lambda b, c: (row(b, c), 1)),
            pl.BlockSpec((L, LANES), lambda b, c: (row(b, c), 0)),
            pl.BlockSpec((LANES, L), lambda b, c: (0, row(b, c))),
            pl.BlockSpec((SSD_CONV, SSD_CONV_CH), const),
            pl.BlockSpec((1, SSD_CONV_CH), const),
            pl.BlockSpec((1, LANES), const),
            pl.BlockSpec((LANES, 1), const),
            pl.BlockSpec((1, SSD_INNER), const),
            pl.BlockSpec((1, SSD_INNER), const),
        ],
        out_specs=pl.BlockSpec((L, SSD_INNER), lambda b, c: (row(b, c), 0)),
        out_shape=jax.ShapeDtypeStruct((TOKENS, SSD_INNER), BF16),
        scratch_shapes=[
            pltpu.VMEM((8, SSD_INNER), F32),
            pltpu.VMEM((8, SSD_BC_WIDTH), F32),
            pltpu.VMEM((8, SSD_BC_WIDTH), F32),
            pltpu.VMEM((SSD_GROUPS, SSD_STATE, SSD_GROUP_WIDTH), F32),
            pltpu.VMEM((L, SSD_INNER), F32),
            pltpu.VMEM((SSD_GROUPS, L, SSD_HEADS_PER_GROUP), F32),
            pltpu.VMEM((SSD_GROUPS, L, SSD_HEADS_PER_GROUP), F32),
            pltpu.VMEM((LANES, L), F32),
        ],
        compiler_params=_params("arbitrary", "arbitrary"),
        name="ssd",
    )(p_main, p_main, p_main, p_main, dt, dt_t,
      conv_w, conv_b.reshape(1, -1), a_pad.reshape(1, LANES), a_pad.reshape(LANES, 1),
      jnp.repeat(d_skip, SSD_HEAD_DIM).reshape(1, SSD_INNER), norm_w.reshape(1, SSD_INNER))


def _branch_kernel(a_ref, s_ref, wa_ref, ws_ref, g0_ref, g1_ref, o_ref, wab, wsb):
    @pl.when(pl.program_id(1) == 0)
    def _():
        wab[...] = wa_ref[...].astype(BF16)
        wsb[...] = ws_ref[...].astype(BF16)

    ya = jnp.dot(a_ref[...], wab[...], preferred_element_type=F32)
    ys = jnp.dot(s_ref[...], wsb[...], preferred_element_type=F32)
    merged = g0_ref[...].astype(F32) * ya + g1_ref[...].astype(F32) * ys
    o_ref[...] = merged.astype(o_ref.dtype)


def _branches(attn, ssd, w_attn, w_ssd, gates):
    tn = 512
    nj = D_MODEL // tn
    return pl.pallas_call(
        _branch_kernel,
        grid=(nj, TOKENS // ROW_TILE),
        in_specs=[pl.BlockSpec((ROW_TILE, GROUP_WIDTH), lambda n, i: (i, 0)),
                  pl.BlockSpec((ROW_TILE, SSD_INNER), lambda n, i: (i, 0)),
                  pl.BlockSpec((None, GROUP_WIDTH, tn), lambda n, i: (0, 0, n)),
                  pl.BlockSpec((None, SSD_INNER, tn), lambda n, i: (0, 0, n)),
                  pl.BlockSpec((ROW_TILE, tn), lambda n, i: (i, n)),
                  pl.BlockSpec((ROW_TILE, tn), lambda n, i: (i, nj + n))],
        out_specs=pl.BlockSpec((ROW_TILE, tn), lambda n, i: (i, n)),
        out_shape=jax.ShapeDtypeStruct((TOKENS, D_MODEL), BF16),
        scratch_shapes=[pltpu.VMEM((GROUP_WIDTH, tn), BF16), pltpu.VMEM((SSD_INNER, tn), BF16)],
        compiler_params=_params("arbitrary", "arbitrary"),
        name="branches",
    )(attn, ssd, w_attn, w_ssd, gates, gates)


def _out_kernel(m_ref, w_ref, x_ref, fw_ref, o_ref):
    h = x_ref[...] + jnp.dot(m_ref[...], w_ref[...], preferred_element_type=F32)
    ms = jnp.mean(h * h, axis=-1, keepdims=True)
    o_ref[...] = h * lax.rsqrt(ms + NORM_EPS) * fw_ref[...]


def _out_proj(merged, w_out_bf, x2, final_w):
    tm = 256
    return pl.pallas_call(
        _out_kernel,
        grid=(TOKENS // tm,),
        in_specs=[pl.BlockSpec((tm, D_MODEL), lambda i: (i, 0)),
                  pl.BlockSpec((D_MODEL, D_MODEL), lambda i: (0, 0)),
                  pl.BlockSpec((tm, D_MODEL), lambda i: (i, 0)),
                  pl.BlockSpec((1, D_MODEL), lambda i: (0, 0))],
        out_specs=pl.BlockSpec((tm, D_MODEL), lambda i: (i, 0)),
        out_shape=jax.ShapeDtypeStruct((TOKENS, D_MODEL), F32),
        compiler_params=_params("arbitrary"),
        name="out_proj",
    )(merged, w_out_bf, x2, final_w.reshape(1, D_MODEL))


def kernel(x, positions, norm_w, w_in, conv_w, conv_b, dt_bias, a_log, d_skip, ssd_norm_w,
           w_attn_br, w_ssd_br, gate_b, w_out, final_norm_w):
    assert x.shape == (BATCH, SEQ, D_MODEL) and w_in.shape[0] == 1
    x2 = x.reshape(TOKENS, D_MODEL)
    xn = _rmsnorm(x2, norm_w[0])
    cos, sin = _rope_tables(positions)
    p_main = _inproj(xn, w_in, cos, sin, dilated=False)
    p_dil = _inproj(xn, w_in, cos, sin, dilated=True)
    gates = _gate_proj(xn, w_in, gate_b[0].reshape(1, 2 * D_MODEL))
    dt_b = jnp.zeros((1, LANES), F32).at[0, :SSD_HEADS].set(dt_bias[0])
    dt = _dt_proj(xn, w_in, dt_b)
    attn = _attention(p_main, p_dil)
    ssd = _ssd(p_main, dt, dt.T, conv_w[0], conv_b[0], a_log[0], d_skip[0], ssd_norm_w[0])
    merged = _branches(attn, ssd, w_attn_br, w_ssd_br, gates)
    out = _out_proj(merged, w_out[0].astype(BF16), x2, final_norm_w)
    return out.reshape(BATCH, SEQ, D_MODEL)
```

```python
import functools

import jax
import jax.numpy as jnp
from jax import lax
from jax.experimental import pallas as pl
from jax.experimental.pallas import tpu as pltpu

F32 = jnp.float32
BF16 = jnp.bfloat16
HIGHEST = lax.Precision.HIGHEST

D_MODEL = 2048
BATCH = 4
SEQ = 2048
TOKENS = BATCH * SEQ
NORM_EPS = 1e-5

HEAD_DIM = 128
HEADS_PER_GROUP = 8
GROUP_WIDTH = HEADS_PER_GROUP * HEAD_DIM
ATTN_BLOCK = 128
DILATIONS = (4, 16)
ROPE_DIMS = HEAD_DIM // 4
ROPE_HALF = ROPE_DIMS // 2
ROPE_THETA = 500000.0
Q_SCALE = HEAD_DIM ** -0.5

SSD_INNER = 2 * D_MODEL
SSD_HEAD_DIM = 64
SSD_HEADS = SSD_INNER // SSD_HEAD_DIM
SSD_GROUPS = 8
SSD_HEADS_PER_GROUP = SSD_HEADS // SSD_GROUPS
SSD_STATE = 128
SSD_CONV = 4
SSD_CHUNK = 128
SSD_GROUP_WIDTH = SSD_HEADS_PER_GROUP * SSD_HEAD_DIM
SSD_BC_WIDTH = SSD_GROUPS * SSD_STATE
SSD_CONV_CH = SSD_INNER + 2 * SSD_BC_WIDTH
N_CHUNKS = SEQ // SSD_CHUNK

COL_DT = 3 * 3 * GROUP_WIDTH + GROUP_WIDTH + SSD_INNER + SSD_CONV_CH
COL_GATES = COL_DT + SSD_HEADS

LANES = 128
ROW_TILE = 512
PROJ_ROWS = 1024
COL_TILE = 1024
SUB_COLS = 256
VMEM_LIMIT = 56 * 1024 * 1024


NT_DIMS = (((1,), (1,)), ((), ()))


def _params(*sem):
    return pltpu.CompilerParams(dimension_semantics=sem, vmem_limit_bytes=VMEM_LIMIT)


def _rmsnorm_kernel(x_ref, w_ref, o_ref):
    x = x_ref[...]
    ms = jnp.mean(x * x, axis=-1, keepdims=True)
    o_ref[...] = (x * lax.rsqrt(ms + NORM_EPS) * w_ref[...]).astype(o_ref.dtype)


def _rmsnorm(x2, w):
    return pl.pallas_call(
        _rmsnorm_kernel,
        grid=(TOKENS // ROW_TILE,),
        in_specs=[pl.BlockSpec((ROW_TILE, D_MODEL), lambda i: (i, 0)),
                  pl.BlockSpec((1, D_MODEL), lambda i: (0, 0))],
        out_specs=pl.BlockSpec((ROW_TILE, D_MODEL), lambda i: (i, 0)),
        out_shape=jax.ShapeDtypeStruct((TOKENS, D_MODEL), BF16),
        compiler_params=_params("arbitrary"),
        name="rmsnorm",
    )(x2, w.reshape(1, D_MODEL))


def _rope_table_kernel(pos_ref, invf_ref, cos_ref, sin_ref):
    ang = pos_ref[...].astype(F32) * invf_ref[...]
    lane = lax.broadcasted_iota(jnp.int32, ang.shape, 1)
    sign = jnp.where(lane < ROPE_HALF, -1.0, jnp.where(lane < ROPE_DIMS, 1.0, 0.0))
    cos_ref[...] = jnp.cos(ang)
    sin_ref[...] = jnp.sin(ang) * sign


def _rope_tables(positions):
    inv_freq = ROPE_THETA ** (-jnp.arange(ROPE_HALF, dtype=F32) / ROPE_HALF)
    invf = jnp.zeros((1, LANES), F32).at[0, :ROPE_DIMS].set(jnp.tile(inv_freq, 2))
    rows = 1024
    return pl.pallas_call(
        _rope_table_kernel,
        grid=(TOKENS // rows,),
        in_specs=[pl.BlockSpec((rows, 1), lambda i: (i, 0)),
                  pl.BlockSpec((1, LANES), lambda i: (0, 0))],
        out_specs=[pl.BlockSpec((rows, LANES), lambda i: (i, 0))] * 2,
        out_shape=[jax.ShapeDtypeStruct((TOKENS, LANES), F32)] * 2,
        compiler_params=_params("arbitrary"),
        name="rope_tables",
    )(positions.reshape(TOKENS, 1), invf)


def _silu(v):
    return 0.5 * v * (1.0 + jnp.tanh(0.5 * v))


def _inproj_kernel(x_ref, w_ref, cos_ref, sin_ref, cw_ref, cb_ref, o_ref, wb_ref, stage_ref, *,
                   n_q, n_k, silu_from, conv_from):
    n = pl.program_id(0)
    i = pl.program_id(1)

    @pl.when(i == 0)
    def _():
        wb_ref[...] = w_ref[...].astype(BF16)

    def sub_dot(c):
        return lax.dot_general(x_ref[...], wb_ref[c * SUB_COLS:(c + 1) * SUB_COLS, :], NT_DIMS,
                               preferred_element_type=F32)

    sub_cols = [slice(c * SUB_COLS, (c + 1) * SUB_COLS) for c in range(COL_TILE // SUB_COLS)]

    @pl.when((n >= n_q + n_k) & (n < silu_from))
    def _():
        for c, cols in enumerate(sub_cols):
            o_ref[:, cols] = sub_dot(c).astype(o_ref.dtype)

    @pl.when((n >= silu_from) & (n < conv_from))
    def _():
        for c, cols in enumerate(sub_cols):
            o_ref[:, cols] = _silu(sub_dot(c)).astype(o_ref.dtype)

    top = stage_ref.shape[1] - PROJ_ROWS
    body = slice(top, top + PROJ_ROWS)

    @pl.when(n >= conv_from)
    def _():
        first = (i % (SEQ // PROJ_ROWS)) == 0
        for c, cols in enumerate(sub_cols):
            st = stage_ref.at[c]
            last = st[PROJ_ROWS:PROJ_ROWS + top, :]
            st[0:top, :] = jnp.where(first, 0.0, last)
            st[body, :] = sub_dot(c)
            w = cw_ref[:, cols]
            y = st[body, :] * w[SSD_CONV - 1:SSD_CONV, :] + cb_ref[:, cols]
            for s in range(1, SSD_CONV):
                y = y + st[top - s:top - s + PROJ_ROWS, :] * w[SSD_CONV - 1 - s:SSD_CONV - s, :]
            o_ref[:, cols] = _silu(y).astype(o_ref.dtype)

    @pl.when(n < n_q + n_k)
    def _():
        cos = cos_ref[...]
        sin = sin_ref[...]
        scale = jnp.where(n < n_q, Q_SCALE, 1.0).astype(F32)
        lane = lax.broadcasted_iota(jnp.int32, (PROJ_ROWS, LANES), 1)
        for c in range(COL_TILE // SUB_COLS):
            st = stage_ref.at[c]
            st[body, :] = sub_dot(c)
            for h in range(SUB_COLS // HEAD_DIM):
                t = st[body, h * HEAD_DIM:(h + 1) * HEAD_DIM]
                partner = jnp.where(lane < ROPE_HALF,
                                    pltpu.roll(t, LANES - ROPE_HALF, 1),
                                    pltpu.roll(t, ROPE_HALF, 1))
                col = c * SUB_COLS + h * HEAD_DIM
                o_ref[:, col:col + HEAD_DIM] = ((t * cos + partner * sin) * scale).astype(o_ref.dtype)


def _inproj(xn, w_t, cos, sin, conv_w, conv_b, dilated):
    if dilated:
        n_cols, n_q, n_k, silu_from, conv_from, dtype = 6, 2, 2, 6, 6, F32
        col_map = lambda n: n + 1 + n // 2
    else:
        n_cols, n_q, n_k, silu_from, conv_from, dtype = 14, 1, 1, 3, 8, BF16
        col_map = lambda n: jnp.where(n < 3, 3 * n, n + 6)
    t_spec = pl.BlockSpec((PROJ_ROWS, LANES), lambda n, i: (i, 0))
    conv_col = lambda n, i: (0, jnp.maximum(n - conv_from, 0))
    return pl.pallas_call(
        functools.partial(_inproj_kernel, n_q=n_q, n_k=n_k, silu_from=silu_from, conv_from=conv_from),
        grid=(n_cols, TOKENS // PROJ_ROWS),
        in_specs=[pl.BlockSpec((PROJ_ROWS, D_MODEL), lambda n, i: (i, 0)),
                  pl.BlockSpec((None, COL_TILE, D_MODEL), lambda n, i: (0, col_map(n), 0)),
                  t_spec, t_spec,
                  pl.BlockSpec((SSD_CONV, COL_TILE), conv_col),
                  pl.BlockSpec((1, COL_TILE), conv_col)],
        out_specs=pl.BlockSpec((PROJ_ROWS, COL_TILE), lambda n, i: (i, n)),
        out_shape=jax.ShapeDtypeStruct((TOKENS, n_cols * COL_TILE), dtype),
        scratch_shapes=[pltpu.VMEM((COL_TILE, D_MODEL), BF16),
                        pltpu.VMEM((COL_TILE // SUB_COLS, 8 + PROJ_ROWS, SUB_COLS), F32)],
        compiler_params=_params("arbitrary", "arbitrary"),
        name="inproj_dilated" if dilated else "inproj_main",
    )(xn, w_t, cos, sin, conv_w, conv_b)


def _gate_kernel(x_ref, wa_ref, wb_ref, b_ref, o_ref, w_s):
    @pl.when(pl.program_id(1) == 0)
    def _():
        w_s[0:COL_TILE - SSD_HEADS, :] = wa_ref[SSD_HEADS:COL_TILE, :].astype(BF16)
        w_s[COL_TILE - SSD_HEADS:COL_TILE, :] = wb_ref[...].astype(BF16)

    for c in range(COL_TILE // SUB_COLS):
        cols = slice(c * SUB_COLS, (c + 1) * SUB_COLS)
        acc = lax.dot_general(x_ref[...], w_s[cols, :], NT_DIMS, preferred_element_type=F32)
        o_ref[:, cols] = jax.nn.sigmoid(acc + b_ref[:, cols]).astype(o_ref.dtype)


def _gate_proj(xn, w_t, gate_bias):
    n_out = 2 * D_MODEL
    first = COL_DT // COL_TILE
    per = COL_TILE // SSD_HEADS
    return pl.pallas_call(
        _gate_kernel,
        grid=(n_out // COL_TILE, TOKENS // PROJ_ROWS),
        in_specs=[pl.BlockSpec((PROJ_ROWS, D_MODEL), lambda n, i: (i, 0)),
                  pl.BlockSpec((None, COL_TILE, D_MODEL), lambda n, i: (0, first + n, 0)),
                  pl.BlockSpec((None, SSD_HEADS, D_MODEL), lambda n, i: (0, (first + n + 1) * per, 0)),
                  pl.BlockSpec((1, COL_TILE), lambda n, i: (0, n))],
        out_specs=pl.BlockSpec((PROJ_ROWS, COL_TILE), lambda n, i: (i, n)),
        out_shape=jax.ShapeDtypeStruct((TOKENS, n_out), BF16),
        scratch_shapes=[pltpu.VMEM((COL_TILE, D_MODEL), BF16)],
        compiler_params=_params("arbitrary", "arbitrary"),
        name="gate_proj",
    )(xn, w_t, w_t, gate_bias)


def _dt_kernel(x_ref, w_ref, brow_ref, bcol_ref, dt_ref, dtt_ref):
    x = x_ref[...]
    w = w_ref[...].astype(BF16)
    dt_ref[...] = jax.nn.softplus(
        lax.dot_general(x, w, NT_DIMS, preferred_element_type=F32) + brow_ref[...])
    dtt_ref[...] = jax.nn.softplus(
        lax.dot_general(w, x, NT_DIMS, preferred_element_type=F32) + bcol_ref[...])


def _dt_proj(xn, w_t, dt_bias):
    return pl.pallas_call(
        _dt_kernel,
        grid=(TOKENS // ROW_TILE,),
        in_specs=[pl.BlockSpec((ROW_TILE, D_MODEL), lambda i: (i, 0)),
                  pl.BlockSpec((None, SSD_HEADS, D_MODEL), lambda i: (0, COL_DT // SSD_HEADS, 0)),
                  pl.BlockSpec((1, SSD_HEADS), lambda i: (0, 0)),
                  pl.BlockSpec((SSD_HEADS, 1), lambda i: (0, 0))],
        out_specs=[pl.BlockSpec((ROW_TILE, SSD_HEADS), lambda i: (i, 0)),
                   pl.BlockSpec((SSD_HEADS, ROW_TILE), lambda i: (0, i))],
        out_shape=[jax.ShapeDtypeStruct((TOKENS, SSD_HEADS), F32),
                   jax.ShapeDtypeStruct((SSD_HEADS, TOKENS), F32)],
        compiler_params=_params("arbitrary"),
        name="dt_proj",
    )(xn, w_t, dt_bias.reshape(1, SSD_HEADS), dt_bias.reshape(SSD_HEADS, 1))


def _attn_kernel(q0, k0, v0, q1, k1, v1, q2, k2, v2, z_ref, o_ref, m_s, a_s, d_s, *perm):
    blk = ATTN_BLOCK
    qi = lax.broadcasted_iota(jnp.int32, (blk, blk), 0)
    ki = lax.broadcasted_iota(jnp.int32, (blk, blk), 1)
    mask_cur = ki <= qi
    mask_prev = ki >= qi
    mask_cat = jnp.concatenate([mask_prev, mask_cur], axis=1)
    nt = (((1,), (1,)), ((), ()))

    for src, dst, dil in zip((q1, k1, v1, q2, k2, v2), perm, (DILATIONS[0],) * 3 + (DILATIONS[1],) * 3):
        sub = SEQ // dil
        for r in range(dil):
            dst[r * sub:(r + 1) * sub, :] = src[pl.ds(r, sub, stride=dil), :].astype(BF16)

    def block(q_ref, k_ref, v_ref, lo, has_prev):
        q = q_ref[lo:lo + blk, :]
        if has_prev:
            k = k_ref[lo - blk:lo + blk, :]
            v = v_ref[lo - blk:lo + blk, :]
            mask = mask_cat
        else:
            k = k_ref[lo:lo + blk, :]
            v = v_ref[lo:lo + blk, :]
            mask = mask_cur
        s = lax.dot_general(q, k, nt, preferred_element_type=F32)
        s = jnp.where(mask, s, -jnp.inf)
        m = jnp.max(s, axis=-1, keepdims=True)
        p = jnp.exp(s - m)
        den = jnp.sum(p, axis=-1, keepdims=True)
        acc = jnp.dot(p.astype(BF16), v, preferred_element_type=F32)
        return m, den, acc

    for i in range(SEQ // blk):
        m, den, acc = block(q0, k0, v0, i * blk, i > 0)
        rows = slice(i * blk, (i + 1) * blk)
        m_s[rows, :] = jnp.broadcast_to(m, (blk, LANES))
        d_s[rows, :] = jnp.broadcast_to(den, (blk, LANES))
        a_s[rows, :] = acc

    def merge(rows, m, den, acc):
        m_old = m_s[rows, :]
        m_new = jnp.maximum(m_old, m)
        w_old = jnp.exp(m_old - m_new)
        w_new = jnp.exp(m - m_new)
        a_s[rows, :] = w_old * a_s[rows, :] + w_new * acc
        d_s[rows, :] = w_old * d_s[rows, :] + w_new * den
        m_s[rows, :] = m_new

    for g, dil in enumerate(DILATIONS):
        q_ref, k_ref, v_ref = perm[3 * g:3 * g + 3]
        sub = SEQ // dil
        for r in range(dil):
            for j in range(sub // blk):
                m, den, acc = block(q_ref, k_ref, v_ref, r * sub + j * blk, j > 0)
                merge(pl.ds(j * blk * dil + r, blk, stride=dil), m, den, acc)

    out = a_s[...] / d_s[...]
    o_ref[...] = (out * z_ref[...].astype(F32)).astype(o_ref.dtype)


def _attention(p_main, p_dil):
    hb = HEADS_PER_GROUP

    def spec(sec):
        return pl.BlockSpec((SEQ, HEAD_DIM), lambda b, h: (b, sec * hb + h))

    return pl.pallas_call(
        _attn_kernel,
        grid=(BATCH, hb),
        in_specs=[spec(0), spec(1), spec(2), spec(0), spec(2), spec(4), spec(1), spec(3), spec(5), spec(3)],
        out_specs=pl.BlockSpec((SEQ, HEAD_DIM), lambda b, h: (b, h)),
        out_shape=jax.ShapeDtypeStruct((TOKENS, GROUP_WIDTH), BF16),
        scratch_shapes=[pltpu.VMEM((SEQ, LANES), F32)] * 3 + [pltpu.VMEM((SEQ, HEAD_DIM), BF16)] * 6,
        compiler_params=_params("arbitrary", "arbitrary"),
        name="dilated_attn",
    )(p_main, p_main, p_main, p_dil, p_dil, p_dil, p_dil, p_dil, p_dil, p_main)


def _ssd_kernel(xs_ref, b_ref, c_ref, z_ref, dt_ref, dtt_ref, arow_ref, acol_ref, dskip_ref, nw_ref,
                o_ref, state, y_s, cs3, cst_s, wrow_s, cdec_s):
    L = SSD_CHUNK
    G = SSD_GROUPS
    J = SSD_HEADS_PER_GROUP
    GW = SSD_GROUP_WIDTH
    N = SSD_STATE

    @pl.when(pl.program_id(1) == 0)
    def _():
        state[...] = jnp.zeros_like(state)

    li = lax.broadcasted_iota(jnp.int32, (L, L), 0)
    si = lax.broadcasted_iota(jnp.int32, (L, L), 1)
    tri = li >= si
    a_row = -jnp.exp(arow_ref[...])
    a_col = -jnp.exp(acol_ref[...])
    dtt = dtt_ref[...]
    cs = jnp.dot(tri.astype(F32), dt_ref[...] * a_row, precision=HIGHEST,
                 preferred_element_type=F32)
    cst = jnp.dot(dtt * a_col, (si >= li).astype(F32), precision=HIGHEST,
                  preferred_element_type=F32)
    cs_last = cst[:, L - 1:L]
    cst_s[...] = cst
    wrow_s[...] = dtt * jnp.exp(cs_last - cst)
    cdec_s[...] = jnp.broadcast_to(jnp.exp(cs_last), (SSD_HEADS, LANES))
    for g in range(G):
        cs3[g] = cs[:, g * J:(g + 1) * J]

    lane = lax.broadcasted_iota(jnp.int32, (L, LANES), 1)
    lo_half = lane < SSD_HEAD_DIM

    def split(v):
        zero = jnp.zeros_like(v)
        return jnp.where(lo_half, v, zero), jnp.where(lo_half, zero, v)

    def group(g, carry):
        cols = pl.ds(pl.multiple_of(g * GW, GW), GW)
        ncols = pl.ds(pl.multiple_of(g * N, N), N)
        heads = pl.ds(pl.multiple_of(g * J, J), J)
        x_g = xs_ref[:, cols]
        b_g = b_ref[:, ncols]
        c_g = c_ref[:, ncols]
        cb = lax.dot_general(c_g, b_g, NT_DIMS, preferred_element_type=F32)
        bt = b_g.astype(F32).T
        cf = c_g.astype(F32)
        cs_g = cs3[g]
        cst_g = cst_s[heads, :]
        w_g = wrow_s[heads, :]
        dtt_g = dtt_ref[heads, :]
        cd_g = cdec_s[heads, :]
        st_prev = state[g]
        y_parts = []
        s_parts = []
        for p in range(J // 2):
            lhs_y = []
            lhs_s = []
            for j in (2 * p, 2 * p + 1):
                col = jnp.broadcast_to(cs_g[:, j:j + 1], (L, LANES))
                decay = jnp.exp(jnp.where(tri, col - cst_g[j:j + 1, :], -jnp.inf))
                lhs_y.append((cb * decay * dtt_g[j:j + 1, :]).astype(BF16))
                lhs_y.append((cf * jnp.exp(col)).astype(BF16))
                lhs_s.append((bt * w_g[j:j + 1, :]).astype(BF16))
            lanes_p = slice(p * LANES, (p + 1) * LANES)
            x_lo, x_hi = split(x_g[:, lanes_p])
            s_p = st_prev[:, lanes_p]
            s_lo, s_hi = split(s_p.astype(BF16))
            y_parts.append(jnp.dot(jnp.concatenate(lhs_y, axis=1),
                                   jnp.concatenate([x_lo, s_lo, x_hi, s_hi], axis=0),
                                   preferred_element_type=F32))
            cd = jnp.where(lo_half[0:1, :], cd_g[2 * p:2 * p + 1, :], cd_g[2 * p + 1:2 * p + 2, :])
            s_parts.append(s_p * cd + jnp.dot(jnp.concatenate(lhs_s, axis=1),
                                              jnp.concatenate([x_lo, x_hi], axis=0),
                                              preferred_element_type=F32))
        state[g] = jnp.concatenate(s_parts, axis=1)
        y = jnp.concatenate(y_parts, axis=1) + x_g.astype(F32) * dskip_ref[:, cols]
        y_s[:, cols] = y * z_ref[:, cols].astype(F32)
        return carry

    lax.fori_loop(0, G, group, 0)

    y = y_s[...]
    ms = jnp.mean(y * y, axis=-1, keepdims=True)
    o_ref[...] = (y * lax.rsqrt(ms + NORM_EPS) * nw_ref[...]).astype(o_ref.dtype)


def _ssd(p_main, dt, dt_t, a_log, d_skip, norm_w):
    L = SSD_CHUNK
    row = lambda b, c: b * N_CHUNKS + c
    const = lambda b, c: (0, 0)
    return pl.pallas_call(
        _ssd_kernel,
        grid=(BATCH, N_CHUNKS),
        in_specs=[
            pl.BlockSpec((L, SSD_INNER), lambda b, c: (row(b, c), 2)),
            pl.BlockSpec((L, SSD_BC_WIDTH), lambda b, c: (row(b, c), 12)),
            pl.BlockSpec((L, SSD_BC_WIDTH), lambda b, c: (row(b, c), 13)),
            pl.BlockSpec((L, SSD_INNER), lambda b, c: (row(b, c), 1)),
            pl.BlockSpec((L, SSD_HEADS), lambda b, c: (row(b, c), 0)),
            pl.BlockSpec((SSD_HEADS, L), lambda b, c: (0, row(b, c))),
            pl.BlockSpec((1, SSD_HEADS), const),
            pl.BlockSpec((SSD_HEADS, 1), const),
            pl.BlockSpec((1, SSD_INNER), const),
            pl.BlockSpec((1, SSD_INNER), const),
        ],
        out_specs=pl.BlockSpec((L, SSD_INNER), lambda b, c: (row(b, c), 0)),
        out_shape=jax.ShapeDtypeStruct((TOKENS, SSD_INNER), BF16),
        scratch_shapes=[
            pltpu.VMEM((SSD_GROUPS, SSD_STATE, SSD_GROUP_WIDTH), F32),
            pltpu.VMEM((L, SSD_INNER), F32),
            pltpu.VMEM((SSD_GROUPS, L, SSD_HEADS_PER_GROUP), F32),
            pltpu.VMEM((SSD_HEADS, L), F32),
            pltpu.VMEM((SSD_HEADS, L), F32),
            pltpu.VMEM((SSD_HEADS, LANES), F32),
        ],
        compiler_params=_params("arbitrary", "arbitrary"),
        name="ssd",
    )(p_main, p_main, p_main, p_main, dt, dt_t,
      a_log.reshape(1, SSD_HEADS), a_log.reshape(SSD_HEADS, 1),
      jnp.repeat(d_skip, SSD_HEAD_DIM).reshape(1, SSD_INNER), norm_w.reshape(1, SSD_INNER))


def _branch_kernel(a_ref, s_ref, wa_ref, ws_ref, g0_ref, g1_ref, o_ref, wab, wsb):
    @pl.when(pl.program_id(1) == 0)
    def _():
        wab[...] = wa_ref[...].astype(BF16)
        wsb[...] = ws_ref[...].astype(BF16)

    for c in range(o_ref.shape[1] // SUB_COLS):
        cols = slice(c * SUB_COLS, (c + 1) * SUB_COLS)
        ya = jnp.dot(a_ref[...], wab[:, cols], preferred_element_type=F32)
        ys = jnp.dot(s_ref[...], wsb[:, cols], preferred_element_type=F32)
        merged = g0_ref[:, cols].astype(F32) * ya + g1_ref[:, cols].astype(F32) * ys
        o_ref[:, cols] = merged.astype(o_ref.dtype)


def _branches(attn, ssd, w_attn, w_ssd, gates):
    tn = 512
    nj = D_MODEL // tn
    return pl.pallas_call(
        _branch_kernel,
        grid=(nj, TOKENS // ROW_TILE),
        in_specs=[pl.BlockSpec((ROW_TILE, GROUP_WIDTH), lambda n, i: (i, 0)),
                  pl.BlockSpec((ROW_TILE, SSD_INNER), lambda n, i: (i, 0)),
                  pl.BlockSpec((None, GROUP_WIDTH, tn), lambda n, i: (0, 0, n)),
                  pl.BlockSpec((None, SSD_INNER, tn), lambda n, i: (0, 0, n)),
                  pl.BlockSpec((ROW_TILE, tn), lambda n, i: (i, n)),
                  pl.BlockSpec((ROW_TILE, tn), lambda n, i: (i, nj + n))],
        out_specs=pl.BlockSpec((ROW_TILE, tn), lambda n, i: (i, n)),
        out_shape=jax.ShapeDtypeStruct((TOKENS, D_MODEL), BF16),
        scratch_shapes=[pltpu.VMEM((GROUP_WIDTH, tn), BF16), pltpu.VMEM((SSD_INNER, tn), BF16)],
        compiler_params=_params("arbitrary", "arbitrary"),
        name="branches",
    )(attn, ssd, w_attn, w_ssd, gates, gates)


def _out_kernel(m_ref, w_ref, x_ref, fw_ref, o_ref):
    sub = 2 * SUB_COLS
    ssq = jnp.zeros((o_ref.shape[0], 1), F32)
    for c in range(D_MODEL // sub):
        cols = slice(c * sub, (c + 1) * sub)
        h = x_ref[:, cols] + jnp.dot(m_ref[...], w_ref[:, cols], preferred_element_type=F32)
        ssq = ssq + jnp.sum(h * h, axis=-1, keepdims=True)
        o_ref[:, cols] = h
    o_ref[...] = o_ref[...] * lax.rsqrt(ssq / D_MODEL + NORM_EPS) * fw_ref[...]


def _out_proj(merged, w_out_bf, x2, final_w):
    tm = ROW_TILE
    return pl.pallas_call(
        _out_kernel,
        grid=(TOKENS // tm,),
        in_specs=[pl.BlockSpec((tm, D_MODEL), lambda i: (i, 0)),
                  pl.BlockSpec((D_MODEL, D_MODEL), lambda i: (0, 0)),
                  pl.BlockSpec((tm, D_MODEL), lambda i: (i, 0)),
                  pl.BlockSpec((1, D_MODEL), lambda i: (0, 0))],
        out_specs=pl.BlockSpec((tm, D_MODEL), lambda i: (i, 0)),
        out_shape=jax.ShapeDtypeStruct((TOKENS, D_MODEL), F32),
        compiler_params=_params("arbitrary"),
        name="out_proj",
    )(merged, w_out_bf, x2, final_w.reshape(1, D_MODEL))


def kernel(x, positions, norm_w, w_in, conv_w, conv_b, dt_bias, a_log, d_skip, ssd_norm_w,
           w_attn_br, w_ssd_br, gate_b, w_out, final_norm_w):
    assert x.shape == (BATCH, SEQ, D_MODEL) and w_in.shape[0] == 1
    x2 = x.reshape(TOKENS, D_MODEL)
    xn = _rmsnorm(x2, norm_w[0])
    cos, sin = _rope_tables(positions)
    w_t = jnp.swapaxes(w_in, 1, 2)
    cw, cb = conv_w[0], conv_b.reshape(1, SSD_CONV_CH)
    p_main = _inproj(xn, w_t, cos, sin, cw, cb, dilated=False)
    p_dil = _inproj(xn, w_t, cos, sin, cw, cb, dilated=True)
    gates = _gate_proj(xn, w_t, gate_b[0].reshape(1, 2 * D_MODEL))
    dt, dt_t = _dt_proj(xn, w_t, dt_bias[0])
    attn = _attention(p_main, p_dil)
    ssd = _ssd(p_main, dt, dt_t, a_log[0], d_skip[0], ssd_norm_w[0])
    merged = _branches(attn, ssd, w_attn_br, w_ssd_br, gates)
    out = _out_proj(merged, w_out[0].astype(BF16), x2, final_norm_w)
    return out.reshape(BATCH, SEQ, D_MODEL)
```

```python
import functools

import jax
import jax.numpy as jnp
from jax import lax
from jax.experimental import pallas as pl
from jax.experimental.pallas import tpu as pltpu

F32 = jnp.float32
BF16 = jnp.bfloat16
HIGHEST = lax.Precision.HIGHEST

D_MODEL = 2048
BATCH = 4
SEQ = 2048
TOKENS = BATCH * SEQ
NORM_EPS = 1e-5

HEAD_DIM = 128
HEADS_PER_GROUP = 8
GROUP_WIDTH = HEADS_PER_GROUP * HEAD_DIM
ATTN_BLOCK = 128
DILATIONS = (4, 16)
ROPE_DIMS = HEAD_DIM // 4
ROPE_HALF = ROPE_DIMS // 2
ROPE_THETA = 500000.0
Q_SCALE = HEAD_DIM ** -0.5
LOG2_E = 1.4426950408889634

SSD_INNER = 2 * D_MODEL
SSD_HEAD_DIM = 64
SSD_HEADS = SSD_INNER // SSD_HEAD_DIM
SSD_GROUPS = 8
SSD_HEADS_PER_GROUP = SSD_HEADS // SSD_GROUPS
SSD_STATE = 128
SSD_CONV = 4
SSD_CHUNK = 128
SSD_GROUP_WIDTH = SSD_HEADS_PER_GROUP * SSD_HEAD_DIM
SSD_BC_WIDTH = SSD_GROUPS * SSD_STATE
SSD_CONV_CH = SSD_INNER + 2 * SSD_BC_WIDTH
N_CHUNKS = SEQ // SSD_CHUNK

COL_DT = 3 * 3 * GROUP_WIDTH + GROUP_WIDTH + SSD_INNER + SSD_CONV_CH
COL_GATES = COL_DT + SSD_HEADS

LANES = 128
ROW_TILE = 512
PROJ_ROWS = 1024
COL_TILE = 1024
SUB_COLS = 256
VMEM_LIMIT = 56 * 1024 * 1024


NT_DIMS = (((1,), (1,)), ((), ()))


def _params(*sem, flags=None):
    return pltpu.CompilerParams(dimension_semantics=sem, vmem_limit_bytes=VMEM_LIMIT, flags=flags)


def _rmsnorm_kernel(x_ref, w_ref, o_ref):
    x = x_ref[...]
    ms = jnp.mean(x * x, axis=-1, keepdims=True)
    o_ref[...] = (x * lax.rsqrt(ms + NORM_EPS) * w_ref[...]).astype(o_ref.dtype)


def _rmsnorm(x2, w):
    return pl.pallas_call(
        _rmsnorm_kernel,
        grid=(TOKENS // ROW_TILE,),
        in_specs=[pl.BlockSpec((ROW_TILE, D_MODEL), lambda i: (i, 0)),
                  pl.BlockSpec((1, D_MODEL), lambda i: (0, 0))],
        out_specs=pl.BlockSpec((ROW_TILE, D_MODEL), lambda i: (i, 0)),
        out_shape=jax.ShapeDtypeStruct((TOKENS, D_MODEL), BF16),
        compiler_params=_params("arbitrary"),
        name="rmsnorm",
    )(x2, w.reshape(1, D_MODEL))


def _rope_table_kernel(pos_ref, invf_ref, cos_ref, sin_ref):
    ang = pos_ref[...].astype(F32) * invf_ref[...]
    lane = lax.broadcasted_iota(jnp.int32, ang.shape, 1)
    sign = jnp.where(lane < ROPE_HALF, -1.0, jnp.where(lane < ROPE_DIMS, 1.0, 0.0))
    cos_ref[...] = jnp.cos(ang)
    sin_ref[...] = jnp.sin(ang) * sign


def _rope_tables(positions):
    inv_freq = ROPE_THETA ** (-jnp.arange(ROPE_HALF, dtype=F32) / ROPE_HALF)
    invf = jnp.zeros((1, LANES), F32).at[0, :ROPE_DIMS].set(jnp.tile(inv_freq, 2))
    rows = 1024
    return pl.pallas_call(
        _rope_table_kernel,
        grid=(TOKENS // rows,),
        in_specs=[pl.BlockSpec((rows, 1), lambda i: (i, 0)),
                  pl.BlockSpec((1, LANES), lambda i: (0, 0))],
        out_specs=[pl.BlockSpec((rows, LANES), lambda i: (i, 0))] * 2,
        out_shape=[jax.ShapeDtypeStruct((TOKENS, LANES), F32)] * 2,
        compiler_params=_params("arbitrary"),
        name="rope_tables",
    )(positions.reshape(TOKENS, 1), invf)


def _silu(v):
    return 0.5 * v * (1.0 + jnp.tanh(0.5 * v))


def _inproj_kernel(x_ref, w_ref, cos_ref, sin_ref, cw_ref, cb_ref, o_ref, wb_ref, stage_a, stage_b, *,
                   n_q, n_k, silu_from, conv_from):
    n = pl.program_id(0)
    i = pl.program_id(1)

    @pl.when(i == 0)
    def _():
        wb_ref[...] = w_ref[...].astype(BF16)

    def sub_dot(c):
        return lax.dot_general(x_ref[...], wb_ref[c * SUB_COLS:(c + 1) * SUB_COLS, :], NT_DIMS,
                               preferred_element_type=F32)

    sub_cols = [slice(c * SUB_COLS, (c + 1) * SUB_COLS) for c in range(COL_TILE // SUB_COLS)]

    @pl.when((n >= n_q + n_k) & (n < silu_from))
    def _():
        for c, cols in enumerate(sub_cols):
            o_ref[:, cols] = sub_dot(c).astype(o_ref.dtype)

    @pl.when((n >= silu_from) & (n < conv_from))
    def _():
        for c, cols in enumerate(sub_cols):
            o_ref[:, cols] = _silu(sub_dot(c)).astype(o_ref.dtype)

    stages = (stage_a, stage_b)
    top = stage_a.shape[1] - PROJ_ROWS
    body = slice(top, top + PROJ_ROWS)
    slot0 = lax.shift_right_logical(i, 30)

    @pl.when(n >= conv_from)
    def _():
        first = (i % (SEQ // PROJ_ROWS)) == 0
        for c, cols in enumerate(sub_cols):
            st = stages[c % 2].at[c // 2]
            last = st[PROJ_ROWS:PROJ_ROWS + top, :]
            st[0:top, :] = jnp.where(first, 0.0, last)
            st[body, :] = sub_dot(c)
            rd = stages[c % 2].at[slot0 + c // 2]
            w = cw_ref[:, cols]
            y = rd[body, :] * w[SSD_CONV - 1:SSD_CONV, :] + cb_ref[:, cols]
            for s in range(1, SSD_CONV):
                y = y + rd[top - s:top - s + PROJ_ROWS, :] * w[SSD_CONV - 1 - s:SSD_CONV - s, :]
            o_ref[:, cols] = _silu(y).astype(o_ref.dtype)

    @pl.when(n < n_q + n_k)
    def _():
        cos = cos_ref[...]
        sin = sin_ref[...]
        scale = jnp.where(n < n_q, Q_SCALE, 1.0).astype(F32)
        lane = lax.broadcasted_iota(jnp.int32, (PROJ_ROWS, LANES), 1)
        for c in range(COL_TILE // SUB_COLS):
            acc = sub_dot(c)
            for h in range(SUB_COLS // HEAD_DIM):
                t = acc[:, h * HEAD_DIM:(h + 1) * HEAD_DIM]
                partner = jnp.where(lane < ROPE_HALF,
                                    pltpu.roll(t, LANES - ROPE_HALF, 1),
                                    pltpu.roll(t, ROPE_HALF, 1))
                col = c * SUB_COLS + h * HEAD_DIM
                o_ref[:, col:col + HEAD_DIM] = ((t * cos + partner * sin) * scale).astype(o_ref.dtype)


def _inproj(xn, w_t, cos, sin, conv_w, conv_b, dilated):
    if dilated:
        n_cols, n_q, n_k, silu_from, conv_from, dtype = 6, 2, 2, 6, 6, F32
        col_map = lambda n: n + 1 + n // 2
    else:
        n_cols, n_q, n_k, silu_from, conv_from, dtype = 14, 1, 1, 3, 8, BF16
        col_map = lambda n: jnp.where(n < 3, 3 * n, n + 6)
    t_spec = pl.BlockSpec((PROJ_ROWS, LANES), lambda n, i: (i, 0))
    conv_col = lambda n, i: (0, jnp.maximum(n - conv_from, 0))
    return pl.pallas_call(
        functools.partial(_inproj_kernel, n_q=n_q, n_k=n_k, silu_from=silu_from, conv_from=conv_from),
        grid=(n_cols, TOKENS // PROJ_ROWS),
        in_specs=[pl.BlockSpec((PROJ_ROWS, D_MODEL), lambda n, i: (i, 0)),
                  pl.BlockSpec((None, COL_TILE, D_MODEL), lambda n, i: (0, col_map(n), 0)),
                  t_spec, t_spec,
                  pl.BlockSpec((SSD_CONV, COL_TILE), conv_col),
                  pl.BlockSpec((1, COL_TILE), conv_col)],
        out_specs=pl.BlockSpec((PROJ_ROWS, COL_TILE), lambda n, i: (i, n)),
        out_shape=jax.ShapeDtypeStruct((TOKENS, n_cols * COL_TILE), dtype),
        scratch_shapes=[pltpu.VMEM((COL_TILE, D_MODEL), BF16),
                        pltpu.VMEM((COL_TILE // SUB_COLS // 2, 8 + PROJ_ROWS, SUB_COLS), F32),
                        pltpu.VMEM((COL_TILE // SUB_COLS // 2, 8 + PROJ_ROWS, SUB_COLS), F32)],
        compiler_params=_params("arbitrary", "arbitrary"),
        name="inproj_dilated" if dilated else "inproj_main",
    )(xn, w_t, cos, sin, conv_w, conv_b)


def _gate_kernel(x_ref, wa_ref, wb_ref, b_ref, o_ref, w_s):
    @pl.when(pl.program_id(1) == 0)
    def _():
        w_s[0:COL_TILE - SSD_HEADS, :] = wa_ref[SSD_HEADS:COL_TILE, :].astype(BF16)
        w_s[COL_TILE - SSD_HEADS:COL_TILE, :] = wb_ref[...].astype(BF16)

    for c in range(COL_TILE // SUB_COLS):
        cols = slice(c * SUB_COLS, (c + 1) * SUB_COLS)
        acc = lax.dot_general(x_ref[...], w_s[cols, :], NT_DIMS, preferred_element_type=F32)
        o_ref[:, cols] = jax.nn.sigmoid(acc + b_ref[:, cols]).astype(o_ref.dtype)


def _gate_proj(xn, w_t, gate_bias):
    n_out = 2 * D_MODEL
    first = COL_DT // COL_TILE
    per = COL_TILE // SSD_HEADS
    return pl.pallas_call(
        _gate_kernel,
        grid=(n_out // COL_TILE, TOKENS // PROJ_ROWS),
        in_specs=[pl.BlockSpec((PROJ_ROWS, D_MODEL), lambda n, i: (i, 0)),
                  pl.BlockSpec((None, COL_TILE, D_MODEL), lambda n, i: (0, first + n, 0)),
                  pl.BlockSpec((None, SSD_HEADS, D_MODEL), lambda n, i: (0, (first + n + 1) * per, 0)),
                  pl.BlockSpec((1, COL_TILE), lambda n, i: (0, n))],
        out_specs=pl.BlockSpec((PROJ_ROWS, COL_TILE), lambda n, i: (i, n)),
        out_shape=jax.ShapeDtypeStruct((TOKENS, n_out), BF16),
        scratch_shapes=[pltpu.VMEM((COL_TILE, D_MODEL), BF16)],
        compiler_params=_params("arbitrary", "arbitrary"),
        name="gate_proj",
    )(xn, w_t, w_t, gate_bias)


def _dt_kernel(x_ref, w_ref, brow_ref, bcol_ref, dt_ref, dtt_ref):
    x = x_ref[...]
    w = w_ref[...].astype(BF16)
    dt_ref[...] = jax.nn.softplus(
        lax.dot_general(x, w, NT_DIMS, preferred_element_type=F32) + brow_ref[...])
    dtt_ref[...] = jax.nn.softplus(
        lax.dot_general(w, x, NT_DIMS, preferred_element_type=F32) + bcol_ref[...])


def _dt_proj(xn, w_t, dt_bias):
    return pl.pallas_call(
        _dt_kernel,
        grid=(TOKENS // ROW_TILE,),
        in_specs=[pl.BlockSpec((ROW_TILE, D_MODEL), lambda i: (i, 0)),
                  pl.BlockSpec((None, SSD_HEADS, D_MODEL), lambda i: (0, COL_DT // SSD_HEADS, 0)),
                  pl.BlockSpec((1, SSD_HEADS), lambda i: (0, 0)),
                  pl.BlockSpec((SSD_HEADS, 1), lambda i: (0, 0))],
        out_specs=[pl.BlockSpec((ROW_TILE, SSD_HEADS), lambda i: (i, 0)),
                   pl.BlockSpec((SSD_HEADS, ROW_TILE), lambda i: (0, i))],
        out_shape=[jax.ShapeDtypeStruct((TOKENS, SSD_HEADS), F32),
                   jax.ShapeDtypeStruct((SSD_HEADS, TOKENS), F32)],
        compiler_params=_params("arbitrary"),
        name="dt_proj",
    )(xn, w_t, dt_bias.reshape(1, SSD_HEADS), dt_bias.reshape(SSD_HEADS, 1))


def _attn_kernel(q0, k0, v0, q1, k1, v1, q2, k2, v2, z_ref, o_ref, m_s, a_s, d_s, *perm):
    blk = ATTN_BLOCK
    qi = lax.broadcasted_iota(jnp.int32, (blk, blk), 0)
    ki = lax.broadcasted_iota(jnp.int32, (blk, blk), 1)
    mask_cur = ki <= qi
    mask_prev = ki >= qi
    nb = SEQ // blk
    qk_dims = (((2,), (2,)), ((0,), (0,)))
    pv_dims = (((2,), (1,)), ((0,), (0,)))

    for src, dst, dil in zip((q1, k1, v1, q2, k2, v2), perm, (DILATIONS[0],) * 3 + (DILATIONS[1],) * 3):
        sub = SEQ // dil
        for r in range(dil):
            dst[r * sub:(r + 1) * sub, :] = src[pl.ds(r, sub, stride=dil), :].astype(BF16)

    def group(q_ref, k_ref, v_ref, blocks_per_seq):
        q = q_ref[...].reshape(nb, blk, HEAD_DIM)
        k = k_ref[...].reshape(nb, blk, HEAD_DIM)
        v = jnp.concatenate([v_ref[...], jnp.ones((SEQ, LANES), BF16)], axis=1).reshape(nb, blk, -1)
        s_cur = lax.dot_general(q, k, qk_dims, preferred_element_type=F32)
        s_cur = jnp.where(mask_cur[None], s_cur, -jnp.inf)
        m = jnp.max(s_cur, axis=-1, keepdims=True)
        if blocks_per_seq > 1:
            k_prev = jnp.concatenate([k[:1], k[:-1]], axis=0)
            v_prev = jnp.concatenate([v[:1], v[:-1]], axis=0)
            b_idx = lax.broadcasted_iota(jnp.int32, (nb, 1, 1), 0)
            no_prev = jnp.where(b_idx % blocks_per_seq == 0, -jnp.inf, 0.0)
            s_prev = lax.dot_general(q, k_prev, qk_dims, preferred_element_type=F32)
            s_prev = jnp.where(mask_prev[None], s_prev, -jnp.inf) + no_prev
            m = jnp.maximum(m, jnp.max(s_prev, axis=-1, keepdims=True))
        p = jnp.exp(s_cur - m)
        acc = lax.dot_general(p.astype(BF16), v, pv_dims, preferred_element_type=F32)
        if blocks_per_seq > 1:
            p = jnp.exp(s_prev - m)
            acc = acc + lax.dot_general(p.astype(BF16), v_prev, pv_dims, preferred_element_type=F32)
        acc = acc.reshape(SEQ, HEAD_DIM + LANES)
        m = jnp.broadcast_to(m, (nb, blk, LANES)).reshape(SEQ, LANES)
        return m, acc[:, HEAD_DIM:], acc[:, :HEAD_DIM]

    for buf, val in zip((m_s, d_s, a_s), group(q0, k0, v0, nb)):
        buf[0] = val
    for g, dil in enumerate(DILATIONS):
        sub = SEQ // dil
        for buf, val in zip((m_s, d_s, a_s), group(*perm[3 * g:3 * g + 3], sub // blk)):
            for r in range(dil):
                buf[g + 1, pl.ds(r, sub, stride=dil), :] = val[r * sub:(r + 1) * sub, :]

    n_groups = len(DILATIONS) + 1
    m_all = [m_s[g] for g in range(n_groups)]
    m_max = functools.reduce(jnp.maximum, m_all)
    w = [jnp.exp(m - m_max) for m in m_all]
    num = sum(w[g] * a_s[g] for g in range(n_groups))
    den = sum(w[g] * d_s[g] for g in range(n_groups))
    o_ref[...] = (num / den * z_ref[...].astype(F32)).astype(o_ref.dtype)


def _attention(p_main, p_dil):
    hb = HEADS_PER_GROUP

    def spec(sec):
        return pl.BlockSpec((SEQ, HEAD_DIM), lambda b, h: (b, sec * hb + h))

    return pl.pallas_call(
        _attn_kernel,
        grid=(BATCH, hb),
        in_specs=[spec(0), spec(1), spec(2), spec(0), spec(2), spec(4), spec(1), spec(3), spec(5), spec(3)],
        out_specs=pl.BlockSpec((SEQ, HEAD_DIM), lambda b, h: (b, h)),
        out_shape=jax.ShapeDtypeStruct((TOKENS, GROUP_WIDTH), BF16),
        scratch_shapes=([pltpu.VMEM((len(DILATIONS) + 1, SEQ, LANES), F32)] * 3
                        + [pltpu.VMEM((SEQ, HEAD_DIM), BF16)] * 6),
        compiler_params=_params("arbitrary", "arbitrary"),
        name="dilated_attn",
    )(p_main, p_main, p_main, p_dil, p_dil, p_dil, p_dil, p_dil, p_dil, p_main)


def _ssd_kernel(xs_ref, b_ref, c_ref, z_ref, dt_ref, dtt_ref, arow_ref, acol_ref, dskip_ref, nw_ref,
                o_ref, state, y_s, cs3, cst_s, wrow_s, cdec_s):
    L = SSD_CHUNK
    G = SSD_GROUPS
    J = SSD_HEADS_PER_GROUP
    GW = SSD_GROUP_WIDTH
    N = SSD_STATE

    @pl.when(pl.program_id(1) == 0)
    def _():
        state[...] = jnp.zeros_like(state)

    li = lax.broadcasted_iota(jnp.int32, (L, L), 0)
    si = lax.broadcasted_iota(jnp.int32, (L, L), 1)
    tri = li >= si
    a_row = -jnp.exp(arow_ref[...])
    a_col = -jnp.exp(acol_ref[...])
    dtt = dtt_ref[...]
    cs = LOG2_E * jnp.dot(tri.astype(F32), dt_ref[...] * a_row, precision=HIGHEST,
                          preferred_element_type=F32)
    cst = LOG2_E * jnp.dot(dtt * a_col, (si >= li).astype(F32), precision=HIGHEST,
                           preferred_element_type=F32)
    cs_last = cst[:, L - 1:L]
    cst_s[...] = cst - jnp.log2(dtt)
    wrow_s[...] = dtt * jnp.exp2(cs_last - cst)
    cdec_s[...] = jnp.broadcast_to(jnp.exp2(cs_last), (SSD_HEADS, LANES))
    for g in range(G):
        cs3[g] = cs[:, g * J:(g + 1) * J]

    lane = lax.broadcasted_iota(jnp.int32, (L, LANES), 1)
    lo_half = lane < SSD_HEAD_DIM

    def split(v):
        zero = jnp.zeros_like(v)
        return jnp.where(lo_half, v, zero), jnp.where(lo_half, zero, v)

    def group(g):
        cols = slice(g * GW, (g + 1) * GW)
        ncols = slice(g * N, (g + 1) * N)
        heads = slice(g * J, (g + 1) * J)
        x_g = xs_ref[:, cols]
        b_g = b_ref[:, ncols]
        c_g = c_ref[:, ncols]
        cb = lax.dot_general(c_g, b_g, NT_DIMS, preferred_element_type=F32)
        bt = b_g.astype(F32).T
        cf = c_g.astype(F32)
        cs_g = cs3[g]
        cst_g = cst_s[heads, :]
        w_g = wrow_s[heads, :]
        cd_g = cdec_s[heads, :]
        st_prev = state[g]
        y_parts = []
        s_parts = []
        for p in range(J // 2):
            lhs_y = []
            lhs_s = []
            for j in (2 * p, 2 * p + 1):
                col = jnp.broadcast_to(cs_g[:, j:j + 1], (L, LANES))
                decay = jnp.exp2(jnp.where(tri, col - cst_g[j:j + 1, :], -jnp.inf))
                lhs_y.append((cb * decay).astype(BF16))
                lhs_y.append((cf * jnp.exp2(col)).astype(BF16))
                lhs_s.append((bt * w_g[j:j + 1, :]).astype(BF16))
            lanes_p = slice(p * LANES, (p + 1) * LANES)
            x_lo, x_hi = split(x_g[:, lanes_p])
            s_p = st_prev[:, lanes_p]
            s_lo, s_hi = split(s_p.astype(BF16))
            y_parts.append(jnp.dot(jnp.concatenate(lhs_y, axis=1),
                                   jnp.concatenate([x_lo, s_lo, x_hi, s_hi], axis=0),
                                   preferred_element_type=F32))
            cd = jnp.where(lo_half[0:1, :], cd_g[2 * p:2 * p + 1, :], cd_g[2 * p + 1:2 * p + 2, :])
            s_parts.append(s_p * cd + jnp.dot(jnp.concatenate(lhs_s, axis=1),
                                              jnp.concatenate([x_lo, x_hi], axis=0),
                                              preferred_element_type=F32))
        state[g] = jnp.concatenate(s_parts, axis=1)
        y = jnp.concatenate(y_parts, axis=1) + x_g.astype(F32) * dskip_ref[:, cols]
        y_s[:, cols] = y * z_ref[:, cols].astype(F32)

    for g in range(G):
        group(g)

    y = y_s[...]
    ms = jnp.mean(y * y, axis=-1, keepdims=True)
    o_ref[...] = (y * lax.rsqrt(ms + NORM_EPS) * nw_ref[...]).astype(o_ref.dtype)


def _ssd(p_main, dt, dt_t, a_log, d_skip, norm_w):
    L = SSD_CHUNK
    row = lambda b, c: b * N_CHUNKS + c
    const = lambda b, c: (0, 0)
    return pl.pallas_call(
        _ssd_kernel,
        grid=(BATCH, N_CHUNKS),
        in_specs=[
            pl.BlockSpec((L, SSD_INNER), lambda b, c: (row(b, c), 2)),
            pl.BlockSpec((L, SSD_BC_WIDTH), lambda b, c: (row(b, c), 12)),
            pl.BlockSpec((L, SSD_BC_WIDTH), lambda b, c: (row(b, c), 13)),
            pl.BlockSpec((L, SSD_INNER), lambda b, c: (row(b, c), 1)),
            pl.BlockSpec((L, SSD_HEADS), lambda b, c: (row(b, c), 0)),
            pl.BlockSpec((SSD_HEADS, L), lambda b, c: (0, row(b, c))),
            pl.BlockSpec((1, SSD_HEADS), const),
            pl.BlockSpec((SSD_HEADS, 1), const),
            pl.BlockSpec((1, SSD_INNER), const),
            pl.BlockSpec((1, SSD_INNER), const),
        ],
        out_specs=pl.BlockSpec((L, SSD_INNER), lambda b, c: (row(b, c), 0)),
        out_shape=jax.ShapeDtypeStruct((TOKENS, SSD_INNER), BF16),
        scratch_shapes=[
            pltpu.VMEM((SSD_GROUPS, SSD_STATE, SSD_GROUP_WIDTH), F32),
            pltpu.VMEM((L, SSD_INNER), F32),
            pltpu.VMEM((SSD_GROUPS, L, SSD_HEADS_PER_GROUP), F32),
            pltpu.VMEM((SSD_HEADS, L), F32),
            pltpu.VMEM((SSD_HEADS, L), F32),
            pltpu.VMEM((SSD_HEADS, LANES), F32),
        ],
        compiler_params=_params("arbitrary", "arbitrary"),
        name="ssd",
    )(p_main, p_main, p_main, p_main, dt, dt_t,
      a_log.reshape(1, SSD_HEADS), a_log.reshape(SSD_HEADS, 1),
      jnp.repeat(d_skip, SSD_HEAD_DIM).reshape(1, SSD_INNER), norm_w.reshape(1, SSD_INNER))


def _branch_kernel(a_ref, s_ref, wa_ref, ws_ref, g0_ref, g1_ref, o_ref, wab, wsb):
    @pl.when(pl.program_id(1) == 0)
    def _():
        wab[...] = wa_ref[...].astype(BF16)
        wsb[...] = ws_ref[...].astype(BF16)

    for c in range(o_ref.shape[1] // SUB_COLS):
        cols = slice(c * SUB_COLS, (c + 1) * SUB_COLS)
        ya = jnp.dot(a_ref[...], wab[:, cols], preferred_element_type=F32)
        ys = jnp.dot(s_ref[...], wsb[:, cols], preferred_element_type=F32)
        merged = g0_ref[:, cols].astype(F32) * ya + g1_ref[:, cols].astype(F32) * ys
        o_ref[:, cols] = merged.astype(o_ref.dtype)


def _branches(attn, ssd, w_attn, w_ssd, gates):
    tn = 512
    nj = D_MODEL // tn
    return pl.pallas_call(
        _branch_kernel,
        grid=(nj, TOKENS // ROW_TILE),
        in_specs=[pl.BlockSpec((ROW_TILE, GROUP_WIDTH), lambda n, i: (i, 0)),
                  pl.BlockSpec((ROW_TILE, SSD_INNER), lambda n, i: (i, 0)),
                  pl.BlockSpec((None, GROUP_WIDTH, tn), lambda n, i: (0, 0, n)),
                  pl.BlockSpec((None, SSD_INNER, tn), lambda n, i: (0, 0, n)),
                  pl.BlockSpec((ROW_TILE, tn), lambda n, i: (i, n)),
                  pl.BlockSpec((ROW_TILE, tn), lambda n, i: (i, nj + n))],
        out_specs=pl.BlockSpec((ROW_TILE, tn), lambda n, i: (i, n)),
        out_shape=jax.ShapeDtypeStruct((TOKENS, D_MODEL), BF16),
        scratch_shapes=[pltpu.VMEM((GROUP_WIDTH, tn), BF16), pltpu.VMEM((SSD_INNER, tn), BF16)],
        compiler_params=_params("arbitrary", "arbitrary"),
        name="branches",
    )(attn, ssd, w_attn, w_ssd, gates, gates)


def _out_kernel(m_ref, w_ref, x_ref, fw_ref, o_ref):
    sub = 2 * SUB_COLS
    ssq = jnp.zeros((o_ref.shape[0], 1), F32)
    for c in range(D_MODEL // sub):
        cols = slice(c * sub, (c + 1) * sub)
        h = x_ref[:, cols] + jnp.dot(m_ref[...], w_ref[:, cols], preferred_element_type=F32)
        ssq = ssq + jnp.sum(h * h, axis=-1, keepdims=True)
        o_ref[:, cols] = h
    o_ref[...] = o_ref[...] * lax.rsqrt(ssq / D_MODEL + NORM_EPS) * fw_ref[...]


def _out_proj(merged, w_out_bf, x2, final_w):
    tm = ROW_TILE
    return pl.pallas_call(
        _out_kernel,
        grid=(TOKENS // tm,),
        in_specs=[pl.BlockSpec((tm, D_MODEL), lambda i: (i, 0)),
                  pl.BlockSpec((D_MODEL, D_MODEL), lambda i: (0, 0)),
                  pl.BlockSpec((tm, D_MODEL), lambda i: (i, 0)),
                  pl.BlockSpec((1, D_MODEL), lambda i: (0, 0))],
        out_specs=pl.BlockSpec((tm, D_MODEL), lambda i: (i, 0)),
        out_shape=jax.ShapeDtypeStruct((TOKENS, D_MODEL), F32),
        compiler_params=_params("arbitrary"),
        name="out_proj",
    )(merged, w_out_bf, x2, final_w.reshape(1, D_MODEL))


def kernel(x, positions, norm_w, w_in, conv_w, conv_b, dt_bias, a_log, d_skip, ssd_norm_w,
           w_attn_br, w_ssd_br, gate_b, w_out, final_norm_w):
    assert x.shape == (BATCH, SEQ, D_MODEL) and w_in.shape[0] == 1
    x2 = x.reshape(TOKENS, D_MODEL)
    xn = _rmsnorm(x2, norm_w[0])
    cos, sin = _rope_tables(positions)
    w_t = jnp.swapaxes(w_in, 1, 2)
    cw, cb = conv_w[0], conv_b.reshape(1, SSD_CONV_CH)
    p_main = _inproj(xn, w_t, cos, sin, cw, cb, dilated=False)
    p_dil = _inproj(xn, w_t, cos, sin, cw, cb, dilated=True)
    gates = _gate_proj(xn, w_t, gate_b[0].reshape(1, 2 * D_MODEL))
    dt, dt_t = _dt_proj(xn, w_t, dt_bias[0])
    attn = _attention(p_main, p_dil)
    ssd = _ssd(p_main, dt, dt_t, a_log[0], d_skip[0], ssd_norm_w[0])
    merged = _branches(attn, ssd, w_attn_br, w_ssd_br, gates)
    out = _out_proj(merged, w_out[0].astype(BF16), x2, final_norm_w)
    return out.reshape(BATCH, SEQ, D_MODEL)
```

```python
import functools

import jax
import jax.numpy as jnp
from jax import lax
from jax.experimental import pallas as pl
from jax.experimental.pallas import tpu as pltpu

F32 = jnp.float32
BF16 = jnp.bfloat16
HIGHEST = lax.Precision.HIGHEST

D_MODEL = 2048
BATCH = 4
SEQ = 2048
TOKENS = BATCH * SEQ
NORM_EPS = 1e-5

HEAD_DIM = 128
HEADS_PER_GROUP = 8
GROUP_WIDTH = HEADS_PER_GROUP * HEAD_DIM
ATTN_BLOCK = 128
DILATIONS = (4, 16)
ROPE_DIMS = HEAD_DIM // 4
ROPE_HALF = ROPE_DIMS // 2
ROPE_THETA = 500000.0
Q_SCALE = HEAD_DIM ** -0.5
LOG2_E = 1.4426950408889634

SSD_INNER = 2 * D_MODEL
SSD_HEAD_DIM = 64
SSD_HEADS = SSD_INNER // SSD_HEAD_DIM
SSD_GROUPS = 8
SSD_HEADS_PER_GROUP = SSD_HEADS // SSD_GROUPS
SSD_STATE = 128
SSD_CONV = 4
SSD_CHUNK = 128
SSD_GROUP_WIDTH = SSD_HEADS_PER_GROUP * SSD_HEAD_DIM
SSD_BC_WIDTH = SSD_GROUPS * SSD_STATE
SSD_CONV_CH = SSD_INNER + 2 * SSD_BC_WIDTH
N_CHUNKS = SEQ // SSD_CHUNK

COL_DT = 3 * 3 * GROUP_WIDTH + GROUP_WIDTH + SSD_INNER + SSD_CONV_CH
COL_GATES = COL_DT + SSD_HEADS

LANES = 128
ROW_TILE = 512
PROJ_ROWS = 1024
COL_TILE = 1024
HALF_TILE = COL_TILE // 2
SUB_COLS = 256
SUBS_PER_HALF = HALF_TILE // SUB_COLS
VMEM_LIMIT = 56 * 1024 * 1024


NT_DIMS = (((1,), (1,)), ((), ()))


def _params(*sem, flags=None):
    return pltpu.CompilerParams(dimension_semantics=sem, vmem_limit_bytes=VMEM_LIMIT, flags=flags)


def _rmsnorm_kernel(x_ref, w_ref, o_ref):
    x = x_ref[...]
    ms = jnp.mean(x * x, axis=-1, keepdims=True)
    o_ref[...] = (x * lax.rsqrt(ms + NORM_EPS) * w_ref[...]).astype(o_ref.dtype)


def _rmsnorm(x2, w):
    return pl.pallas_call(
        _rmsnorm_kernel,
        grid=(TOKENS // ROW_TILE,),
        in_specs=[pl.BlockSpec((ROW_TILE, D_MODEL), lambda i: (i, 0)),
                  pl.BlockSpec((1, D_MODEL), lambda i: (0, 0))],
        out_specs=pl.BlockSpec((ROW_TILE, D_MODEL), lambda i: (i, 0)),
        out_shape=jax.ShapeDtypeStruct((TOKENS, D_MODEL), BF16),
        compiler_params=_params("arbitrary"),
        name="rmsnorm",
    )(x2, w.reshape(1, D_MODEL))


def _rope_table_kernel(pos_ref, invf_ref, cos_ref, sin_ref):
    ang = pos_ref[...].astype(F32) * invf_ref[...]
    lane = lax.broadcasted_iota(jnp.int32, ang.shape, 1)
    sign = jnp.where(lane < ROPE_HALF, -1.0, jnp.where(lane < ROPE_DIMS, 1.0, 0.0))
    cos_ref[...] = jnp.cos(ang)
    sin_ref[...] = jnp.sin(ang) * sign


def _rope_tables(positions):
    inv_freq = ROPE_THETA ** (-jnp.arange(ROPE_HALF, dtype=F32) / ROPE_HALF)
    invf = jnp.zeros((1, LANES), F32).at[0, :ROPE_DIMS].set(jnp.tile(inv_freq, 2))
    rows = 1024
    return pl.pallas_call(
        _rope_table_kernel,
        grid=(TOKENS // rows,),
        in_specs=[pl.BlockSpec((rows, 1), lambda i: (i, 0)),
                  pl.BlockSpec((1, LANES), lambda i: (0, 0))],
        out_specs=[pl.BlockSpec((rows, LANES), lambda i: (i, 0))] * 2,
        out_shape=[jax.ShapeDtypeStruct((TOKENS, LANES), F32)] * 2,
        compiler_params=_params("arbitrary"),
        name="rope_tables",
    )(positions.reshape(TOKENS, 1), invf)


def _silu(v):
    return 0.5 * v * (1.0 + jnp.tanh(0.5 * v))


def _lookup(n, values):
    out = values[-1]
    for k in range(len(values) - 2, -1, -1):
        out = jnp.where(n == k, values[k], out)
    return out


def _inproj_kernel(x_ref, wa_ref, wb_ref, cos_ref, sin_ref, cw_ref, cb_ref, *refs, plan, split_out):
    if split_out:
        oa_ref, ob_ref, w_s, stage_a, stage_b = refs
    else:
        o_ref, w_s, stage_a, stage_b = refs
        oa_ref = o_ref.at[:, 0:HALF_TILE]
        ob_ref = o_ref.at[:, HALF_TILE:COL_TILE]
    n = pl.program_id(0)
    i = pl.program_id(1)

    @pl.when(i == 0)
    def _():
        w_s[0:HALF_TILE, :] = wa_ref[...].astype(BF16)
        w_s[HALF_TILE:COL_TILE, :] = wb_ref[...].astype(BF16)

    def sub_dot(c):
        return lax.dot_general(x_ref[...], w_s[c * SUB_COLS:(c + 1) * SUB_COLS, :], NT_DIMS,
                               preferred_element_type=F32)

    stages = (stage_a, stage_b)
    top = stage_a.shape[1] - PROJ_ROWS
    body = slice(top, top + PROJ_ROWS)
    slot0 = lax.shift_right_logical(i, 30)

    def epilogue(kind, c, o_ref):
        lc = c % SUBS_PER_HALF
        cols = slice(lc * SUB_COLS, (lc + 1) * SUB_COLS)
        if kind == "plain":
            o_ref[:, cols] = sub_dot(c).astype(o_ref.dtype)
        elif kind == "silu":
            o_ref[:, cols] = _silu(sub_dot(c)).astype(o_ref.dtype)
        elif kind == "conv":
            first = (i % (SEQ // PROJ_ROWS)) == 0
            st = stages[c % 2].at[c // 2]
            last = st[PROJ_ROWS:PROJ_ROWS + top, :]
            st[0:top, :] = jnp.where(first, 0.0, last)
            st[body, :] = sub_dot(c)
            rd = stages[c % 2].at[slot0 + c // 2]
            wcols = slice(c * SUB_COLS, (c + 1) * SUB_COLS)
            w = cw_ref[:, wcols]
            y = rd[body, :] * w[SSD_CONV - 1:SSD_CONV, :] + cb_ref[:, wcols]
            for s in range(1, SSD_CONV):
                y = y + rd[top - s:top - s + PROJ_ROWS, :] * w[SSD_CONV - 1 - s:SSD_CONV - s, :]
            o_ref[:, cols] = _silu(y).astype(o_ref.dtype)
        else:
            acc = sub_dot(c)
            cos = cos_ref[...]
            sin = sin_ref[...]
            scale = Q_SCALE if kind == "rope_q" else 1.0
            lane = lax.broadcasted_iota(jnp.int32, (PROJ_ROWS, LANES), 1)
            for h in range(SUB_COLS // HEAD_DIM):
                t = acc[:, h * HEAD_DIM:(h + 1) * HEAD_DIM]
                partner = jnp.where(lane < ROPE_HALF,
                                    pltpu.roll(t, LANES - ROPE_HALF, 1),
                                    pltpu.roll(t, ROPE_HALF, 1))
                col = lc * SUB_COLS + h * HEAD_DIM
                o_ref[:, col:col + HEAD_DIM] = ((t * cos + partner * sin) * scale).astype(o_ref.dtype)

    for lo, hi, kind_a, kind_b in plan:
        @pl.when((n >= lo) & (n < hi))
        def _(kind_a=kind_a, kind_b=kind_b):
            for c in range(SUBS_PER_HALF):
                epilogue(kind_a, c, oa_ref)
            for c in range(SUBS_PER_HALF, 2 * SUBS_PER_HALF):
                epilogue(kind_b, c, ob_ref)


_W_Q, _W_K, _W_V = 0, 6, 12
_W_ZATTN, _W_ZSSD, _W_XBC = 18, 20, 28
_MAIN = dict(
    wa=[_W_Q, _W_Q + 1, _W_K, _W_K + 1] + [_W_ZSSD + 2 * k for k in range(4)]
    + [_W_XBC + 2 * k for k in range(6)],
    wb=[_W_V, _W_V + 1, _W_ZATTN, _W_ZATTN + 1] + [_W_ZSSD + 2 * k + 1 for k in range(4)]
    + [_W_XBC + 2 * k + 1 for k in range(6)],
    plan=((0, 2, "rope_q", "plain"), (2, 4, "rope_k", "silu"), (4, 8, "silu", "silu"),
          (8, 14, "conv", "conv")),
    conv_first=8, split_out=False, dtype=BF16)
_DILATED = dict(
    wa=[_W_Q + 2, _W_Q + 3, _W_Q + 4, _W_Q + 5, _W_K + 2, _W_K + 3],
    wb=[_W_V + 2, _W_V + 3, _W_V + 4, _W_V + 5, _W_K + 4, _W_K + 5],
    plan=((0, 4, "rope_q", "plain"), (4, 6, "rope_k", "rope_k")),
    conv_first=0, split_out=True,
    dtype=F32)


def _inproj(xn, w_t, cos, sin, conv_w, conv_b, cfg, name):
    steps = len(cfg["wa"])
    t_spec = pl.BlockSpec((PROJ_ROWS, LANES), lambda n, i: (i, 0))
    conv_col = lambda n, i: (0, jnp.clip(n - cfg["conv_first"], 0, SSD_CONV_CH // COL_TILE - 1))
    w_spec = lambda rows: pl.BlockSpec((None, HALF_TILE, D_MODEL), lambda n, i: (0, _lookup(n, rows), 0))
    if cfg["split_out"]:
        out_specs = [pl.BlockSpec((PROJ_ROWS, HALF_TILE), lambda n, i: (i, n))] * 2
        out_shape = [jax.ShapeDtypeStruct((TOKENS, steps * HALF_TILE), cfg["dtype"])] * 2
    else:
        out_specs = pl.BlockSpec((PROJ_ROWS, COL_TILE), lambda n, i: (i, n))
        out_shape = jax.ShapeDtypeStruct((TOKENS, steps * COL_TILE), cfg["dtype"])
    stage = pltpu.VMEM((2, 8 + PROJ_ROWS, SUB_COLS), F32)
    return pl.pallas_call(
        functools.partial(_inproj_kernel, plan=cfg["plan"], split_out=cfg["split_out"]),
        grid=(steps, TOKENS // PROJ_ROWS),
        in_specs=[pl.BlockSpec((PROJ_ROWS, D_MODEL), lambda n, i: (i, 0)),
                  w_spec(cfg["wa"]), w_spec(cfg["wb"]), t_spec, t_spec,
                  pl.BlockSpec((SSD_CONV, COL_TILE), conv_col),
                  pl.BlockSpec((1, COL_TILE), conv_col)],
        out_specs=out_specs,
        out_shape=out_shape,
        scratch_shapes=[pltpu.VMEM((COL_TILE, D_MODEL), BF16), stage, stage],
        compiler_params=_params("arbitrary", "arbitrary"),
        name=name,
    )(xn, w_t, w_t, cos, sin, conv_w, conv_b)


def _gate_kernel(x_ref, wa_ref, wb_ref, b_ref, dtb_col_ref, o_ref, dt_ref, dtt_ref, w_s):
    @pl.when(pl.program_id(1) == 0)
    def _():
        w_s[0:COL_TILE - SSD_HEADS, :] = wa_ref[SSD_HEADS:COL_TILE, :].astype(BF16)
        w_s[COL_TILE - SSD_HEADS:COL_TILE, :] = wb_ref[...].astype(BF16)

    @pl.when(pl.program_id(0) == 0)
    def _():
        w = wa_ref[0:SSD_HEADS, :].astype(BF16)
        dtt = jax.nn.softplus(
            lax.dot_general(w, x_ref[...], NT_DIMS, preferred_element_type=F32) + dtb_col_ref[...])
        dtt_ref[...] = dtt
        dt_ref[...] = dtt.T

    for c in range(COL_TILE // SUB_COLS):
        cols = slice(c * SUB_COLS, (c + 1) * SUB_COLS)
        acc = lax.dot_general(x_ref[...], w_s[cols, :], NT_DIMS, preferred_element_type=F32)
        logits = acc + b_ref[:, cols]
        o_ref[:, cols] = (0.5 + 0.5 * jnp.tanh(0.5 * logits)).astype(o_ref.dtype)


def _gate_proj(xn, w_t, gate_bias, dt_bias):
    n_out = 2 * D_MODEL
    first = COL_DT // COL_TILE
    per = COL_TILE // SSD_HEADS
    n_rows = TOKENS // PROJ_ROWS
    dt_row = lambda n, i: jnp.where(n == 0, i, n_rows - 1)
    return pl.pallas_call(
        _gate_kernel,
        grid=(n_out // COL_TILE, n_rows),
        in_specs=[pl.BlockSpec((PROJ_ROWS, D_MODEL), lambda n, i: (i, 0)),
                  pl.BlockSpec((None, COL_TILE, D_MODEL), lambda n, i: (0, first + n, 0)),
                  pl.BlockSpec((None, SSD_HEADS, D_MODEL), lambda n, i: (0, (first + n + 1) * per, 0)),
                  pl.BlockSpec((1, COL_TILE), lambda n, i: (0, n)),
                  pl.BlockSpec((SSD_HEADS, 1), lambda n, i: (0, 0))],
        out_specs=[pl.BlockSpec((PROJ_ROWS, COL_TILE), lambda n, i: (i, n)),
                   pl.BlockSpec((PROJ_ROWS, SSD_HEADS), lambda n, i: (dt_row(n, i), 0)),
                   pl.BlockSpec((SSD_HEADS, PROJ_ROWS), lambda n, i: (0, dt_row(n, i)))],
        out_shape=[jax.ShapeDtypeStruct((TOKENS, n_out), BF16),
                   jax.ShapeDtypeStruct((TOKENS, SSD_HEADS), F32),
                   jax.ShapeDtypeStruct((SSD_HEADS, TOKENS), F32)],
        scratch_shapes=[pltpu.VMEM((COL_TILE, D_MODEL), BF16)],
        compiler_params=_params("arbitrary", "arbitrary"),
        name="gate_proj",
    )(xn, w_t, w_t, gate_bias, dt_bias.reshape(SSD_HEADS, 1))


def _attn_kernel(q0, k0, v0, q1, k1, v1, q2, k2, v2, z_ref, o_ref, m_s, a_s, d_s, *perm):
    blk = ATTN_BLOCK
    qi = lax.broadcasted_iota(jnp.int32, (blk, blk), 0)
    ki = lax.broadcasted_iota(jnp.int32, (blk, blk), 1)
    mask_cur = ki <= qi
    mask_prev = ki >= qi
    nb = SEQ // blk
    qk_dims = (((2,), (2,)), ((0,), (0,)))
    pv_dims = (((2,), (1,)), ((0,), (0,)))

    for src, dst, dil in zip((q1, k1, v1, q2, k2, v2), perm, (DILATIONS[0],) * 3 + (DILATIONS[1],) * 3):
        sub = SEQ // dil
        for r in range(dil):
            dst[r * sub:(r + 1) * sub, :] = src[pl.ds(r, sub, stride=dil), :].astype(BF16)

    def group(q_ref, k_ref, v_ref, blocks_per_seq):
        q = q_ref[...].reshape(nb, blk, HEAD_DIM)
        k = k_ref[...].reshape(nb, blk, HEAD_DIM)
        v = jnp.concatenate([v_ref[...], jnp.ones((SEQ, LANES), BF16)], axis=1).reshape(nb, blk, -1)
        s_cur = lax.dot_general(q, k, qk_dims, preferred_element_type=F32)
        s_cur = jnp.where(mask_cur[None], s_cur, -jnp.inf)
        m = jnp.max(s_cur, axis=-1, keepdims=True)
        if blocks_per_seq > 1:
            k_prev = jnp.concatenate([k[:1], k[:-1]], axis=0)
            v_prev = jnp.concatenate([v[:1], v[:-1]], axis=0)
            b_idx = lax.broadcasted_iota(jnp.int32, (nb, 1, 1), 0)
            no_prev = jnp.where(b_idx % blocks_per_seq == 0, -jnp.inf, 0.0)
            s_prev = lax.dot_general(q, k_prev, qk_dims, preferred_element_type=F32)
            s_prev = jnp.where(mask_prev[None], s_prev, -jnp.inf) + no_prev
            m = jnp.maximum(m, jnp.max(s_prev, axis=-1, keepdims=True))
        p = jnp.exp(s_cur - m)
        acc = lax.dot_general(p.astype(BF16), v, pv_dims, preferred_element_type=F32)
        if blocks_per_seq > 1:
            p = jnp.exp(s_prev - m)
            acc = acc + lax.dot_general(p.astype(BF16), v_prev, pv_dims, preferred_element_type=F32)
        acc = acc.reshape(SEQ, HEAD_DIM + LANES)
        m = jnp.broadcast_to(m, (nb, blk, LANES)).reshape(SEQ, LANES)
        return m, acc[:, HEAD_DIM:], acc[:, :HEAD_DIM]

    for buf, val in zip((m_s, d_s, a_s), group(q0, k0, v0, nb)):
        buf[0] = val
    for g, dil in enumerate(DILATIONS):
        sub = SEQ // dil
        for buf, val in zip((m_s, d_s, a_s), group(*perm[3 * g:3 * g + 3], sub // blk)):
            for r in range(dil):
                buf[g + 1, pl.ds(r, sub, stride=dil), :] = val[r * sub:(r + 1) * sub, :]

    n_groups = len(DILATIONS) + 1
    m_all = [m_s[g] for g in range(n_groups)]
    m_max = functools.reduce(jnp.maximum, m_all)
    w = [jnp.exp(m - m_max) for m in m_all]
    num = sum(w[g] * a_s[g] for g in range(n_groups))
    den = sum(w[g] * d_s[g] for g in range(n_groups))
    o_ref[...] = (num / den * z_ref[...].astype(F32)).astype(o_ref.dtype)


def _attention(p_main, dil_a, dil_b):
    hb = HEADS_PER_GROUP

    def spec(sec):
        return pl.BlockSpec((SEQ, HEAD_DIM), lambda b, h: (b, sec * hb + h))

    def main_spec(first):
        per = HALF_TILE // HEAD_DIM
        return pl.BlockSpec((SEQ, HEAD_DIM), lambda b, h: (b, first * per + h + per * (h // per)))

    return pl.pallas_call(
        _attn_kernel,
        grid=(BATCH, hb),
        in_specs=[main_spec(0), main_spec(4), main_spec(1), spec(0), spec(2), spec(0), spec(1), spec(2),
                  spec(1), main_spec(5)],
        out_specs=pl.BlockSpec((SEQ, HEAD_DIM), lambda b, h: (b, h)),
        out_shape=jax.ShapeDtypeStruct((TOKENS, GROUP_WIDTH), BF16),
        scratch_shapes=([pltpu.VMEM((len(DILATIONS) + 1, SEQ, LANES), F32)] * 3
                        + [pltpu.VMEM((SEQ, HEAD_DIM), BF16)] * 6),
        compiler_params=_params("arbitrary", "arbitrary"),
        name="dilated_attn",
    )(p_main, p_main, p_main, dil_a, dil_a, dil_b, dil_a, dil_b, dil_b, p_main)


def _ssd_kernel(xs_ref, b_ref, c_ref, z_ref, dt_ref, dtt_ref, arow_ref, acol_ref, dskip_ref, nw_ref,
                o_ref, state, y_s, cs3, cst_s, wrow_s, cdec_s):
    L = SSD_CHUNK
    G = SSD_GROUPS
    J = SSD_HEADS_PER_GROUP
    GW = SSD_GROUP_WIDTH
    N = SSD_STATE

    @pl.when(pl.program_id(1) == 0)
    def _():
        state[...] = jnp.zeros_like(state)

    li = lax.broadcasted_iota(jnp.int32, (L, L), 0)
    si = lax.broadcasted_iota(jnp.int32, (L, L), 1)
    tri = li >= si
    a_row = -jnp.exp(arow_ref[...])
    a_col = -jnp.exp(acol_ref[...])
    dtt = dtt_ref[...]
    cs = LOG2_E * jnp.dot(tri.astype(F32), dt_ref[...] * a_row, precision=HIGHEST,
                          preferred_element_type=F32)
    cst = LOG2_E * jnp.dot(dtt * a_col, (si >= li).astype(F32), precision=HIGHEST,
                           preferred_element_type=F32)
    cs_last = cst[:, L - 1:L]
    cst_s[...] = cst - jnp.log2(dtt)
    wrow_s[...] = dtt * jnp.exp2(cs_last - cst)
    cdec_s[...] = jnp.broadcast_to(jnp.exp2(cs_last), (SSD_HEADS, LANES))
    for g in range(G):
        cs3[g] = cs[:, g * J:(g + 1) * J]

    lane = lax.broadcasted_iota(jnp.int32, (L, LANES), 1)
    lo_half = lane < SSD_HEAD_DIM

    def split(v):
        zero = jnp.zeros_like(v)
        return jnp.where(lo_half, v, zero), jnp.where(lo_half, zero, v)

    def group(g):
        cols = slice(g * GW, (g + 1) * GW)
        ncols = slice(g * N, (g + 1) * N)
        heads = slice(g * J, (g + 1) * J)
        x_g = xs_ref[:, cols]
        b_g = b_ref[:, ncols]
        c_g = c_ref[:, ncols]
        cb = lax.dot_general(c_g, b_g, NT_DIMS, preferred_element_type=F32)
        bt = b_g.astype(F32).T
        cf = c_g.astype(F32)
        cs_g = cs3[g]
        cst_g = cst_s[heads, :]
        w_g = wrow_s[heads, :]
        cd_g = cdec_s[heads, :]
        st_prev = state[g]
        y_parts = []
        s_parts = []
        for p in range(J // 2):
            lhs_y = []
            lhs_s = []
            for j in (2 * p, 2 * p + 1):
                col = jnp.broadcast_to(cs_g[:, j:j + 1], (L, LANES))
                decay = jnp.exp2(jnp.where(tri, col - cst_g[j:j + 1, :], -jnp.inf))
                lhs_y.append((cb * decay).astype(BF16))
                lhs_y.append((cf * jnp.exp2(col)).astype(BF16))
                lhs_s.append((bt * w_g[j:j + 1, :]).astype(BF16))
            lanes_p = slice(p * LANES, (p + 1) * LANES)
            x_lo, x_hi = split(x_g[:, lanes_p])
            s_p = st_prev[:, lanes_p]
            s_lo, s_hi = split(s_p.astype(BF16))
            y_parts.append(jnp.dot(jnp.concatenate(lhs_y, axis=1),
                                   jnp.concatenate([x_lo, s_lo, x_hi, s_hi], axis=0),
                                   preferred_element_type=F32))
            cd = jnp.where(lo_half[0:1, :], cd_g[2 * p:2 * p + 1, :], cd_g[2 * p + 1:2 * p + 2, :])
            s_parts.append(s_p * cd + jnp.dot(jnp.concatenate(lhs_s, axis=1),
                                              jnp.concatenate([x_lo, x_hi], axis=0),
                                              preferred_element_type=F32))
        state[g] = jnp.concatenate(s_parts, axis=1)
        y = jnp.concatenate(y_parts, axis=1) + x_g.astype(F32) * dskip_ref[:, cols]
        y_s[:, cols] = y * z_ref[:, cols].astype(F32)

    for g in range(G):
        group(g)

    y = y_s[...]
    ms = jnp.mean(y * y, axis=-1, keepdims=True)
    o_ref[...] = (y * lax.rsqrt(ms + NORM_EPS) * nw_ref[...]).astype(o_ref.dtype)


def _ssd(p_main, dt, dt_t, a_log, d_skip, norm_w):
    L = SSD_CHUNK
    row = lambda b, c: b * N_CHUNKS + c
    const = lambda b, c: (0, 0)
    return pl.pallas_call(
        _ssd_kernel,
        grid=(BATCH, N_CHUNKS),
        in_specs=[
            pl.BlockSpec((L, SSD_INNER), lambda b, c: (row(b, c), 2)),
            pl.BlockSpec((L, SSD_BC_WIDTH), lambda b, c: (row(b, c), 12)),
            pl.BlockSpec((L, SSD_BC_WIDTH), lambda b, c: (row(b, c), 13)),
            pl.BlockSpec((L, SSD_INNER), lambda b, c: (row(b, c), 1)),
            pl.BlockSpec((L, SSD_HEADS), lambda b, c: (row(b, c), 0)),
            pl.BlockSpec((SSD_HEADS, L), lambda b, c: (0, row(b, c))),
            pl.BlockSpec((1, SSD_HEADS), const),
            pl.BlockSpec((SSD_HEADS, 1), const),
            pl.BlockSpec((1, SSD_INNER), const),
            pl.BlockSpec((1, SSD_INNER), const),
        ],
        out_specs=pl.BlockSpec((L, SSD_INNER), lambda b, c: (row(b, c), 0)),
        out_shape=jax.ShapeDtypeStruct((TOKENS, SSD_INNER), BF16),
        scratch_shapes=[
            pltpu.VMEM((SSD_GROUPS, SSD_STATE, SSD_GROUP_WIDTH), F32),
            pltpu.VMEM((L, SSD_INNER), F32),
            pltpu.VMEM((SSD_GROUPS, L, SSD_HEADS_PER_GROUP), F32),
            pltpu.VMEM((SSD_HEADS, L), F32),
            pltpu.VMEM((SSD_HEADS, L), F32),
            pltpu.VMEM((SSD_HEADS, LANES), F32),
        ],
        compiler_params=_params("arbitrary", "arbitrary"),
        name="ssd",
    )(p_main, p_main, p_main, p_main, dt, dt_t,
      a_log.reshape(1, SSD_HEADS), a_log.reshape(SSD_HEADS, 1),
      jnp.repeat(d_skip, SSD_HEAD_DIM).reshape(1, SSD_INNER), norm_w.reshape(1, SSD_INNER))


def _branch_kernel(a_ref, s_ref, wa_ref, ws_ref, g0_ref, g1_ref, o_ref, wab, wsb):
    @pl.when(pl.program_id(1) == 0)
    def _():
        wab[...] = wa_ref[...].astype(BF16)
        wsb[...] = ws_ref[...].astype(BF16)

    for c in range(o_ref.shape[1] // SUB_COLS):
        cols = slice(c * SUB_COLS, (c + 1) * SUB_COLS)
        ya = jnp.dot(a_ref[...], wab[:, cols], preferred_element_type=F32)
        ys = jnp.dot(s_ref[...], wsb[:, cols], preferred_element_type=F32)
        merged = g0_ref[:, cols].astype(F32) * ya + g1_ref[:, cols].astype(F32) * ys
        o_ref[:, cols] = merged.astype(o_ref.dtype)


def _branches(attn, ssd, w_attn, w_ssd, gates):
    tn = 512
    nj = D_MODEL // tn
    return pl.pallas_call(
        _branch_kernel,
        grid=(nj, TOKENS // ROW_TILE),
        in_specs=[pl.BlockSpec((ROW_TILE, GROUP_WIDTH), lambda n, i: (i, 0)),
                  pl.BlockSpec((ROW_TILE, SSD_INNER), lambda n, i: (i, 0)),
                  pl.BlockSpec((None, GROUP_WIDTH, tn), lambda n, i: (0, 0, n)),
                  pl.BlockSpec((None, SSD_INNER, tn), lambda n, i: (0, 0, n)),
                  pl.BlockSpec((ROW_TILE, tn), lambda n, i: (i, n)),
                  pl.BlockSpec((ROW_TILE, tn), lambda n, i: (i, nj + n))],
        out_specs=pl.BlockSpec((ROW_TILE, tn), lambda n, i: (i, n)),
        out_shape=jax.ShapeDtypeStruct((TOKENS, D_MODEL), BF16),
        scratch_shapes=[pltpu.VMEM((GROUP_WIDTH, tn), BF16), pltpu.VMEM((SSD_INNER, tn), BF16)],
        compiler_params=_params("arbitrary", "arbitrary"),
        name="branches",
    )(attn, ssd, w_attn, w_ssd, gates, gates)


def _out_kernel(m_ref, w_ref, x_ref, fw_ref, o_ref):
    sub = 2 * SUB_COLS
    ssq = jnp.zeros((o_ref.shape[0], 1), F32)
    for c in range(D_MODEL // sub):
        cols = slice(c * sub, (c + 1) * sub)
        h = x_ref[:, cols] + jnp.dot(m_ref[...], w_ref[:, cols], preferred_element_type=F32)
        ssq = ssq + jnp.sum(h * h, axis=-1, keepdims=True)
        o_ref[:, cols] = h
    o_ref[...] = o_ref[...] * lax.rsqrt(ssq / D_MODEL + NORM_EPS) * fw_ref[...]


def _out_proj(merged, w_out_bf, x2, final_w):
    tm = ROW_TILE
    return pl.pallas_call(
        _out_kernel,
        grid=(TOKENS // tm,),
        in_specs=[pl.BlockSpec((tm, D_MODEL), lambda i: (i, 0)),
                  pl.BlockSpec((D_MODEL, D_MODEL), lambda i: (0, 0)),
                  pl.BlockSpec((tm, D_MODEL), lambda i: (i, 0)),
                  pl.BlockSpec((1, D_MODEL), lambda i: (0, 0))],
        out_specs=pl.BlockSpec((tm, D_MODEL), lambda i: (i, 0)),
        out_shape=jax.ShapeDtypeStruct((TOKENS, D_MODEL), F32),
        compiler_params=_params("arbitrary"),
        name="out_proj",
    )(merged, w_out_bf, x2, final_w.reshape(1, D_MODEL))


def kernel(x, positions, norm_w, w_in, conv_w, conv_b, dt_bias, a_log, d_skip, ssd_norm_w,
           w_attn_br, w_ssd_br, gate_b, w_out, final_norm_w):
    assert x.shape == (BATCH, SEQ, D_MODEL) and w_in.shape[0] == 1
    x2 = x.reshape(TOKENS, D_MODEL)
    xn = _rmsnorm(x2, norm_w[0])
    cos, sin = _rope_tables(positions)
    w_t = jnp.swapaxes(w_in, 1, 2)
    cw, cb = conv_w[0], conv_b.reshape(1, SSD_CONV_CH)
    p_main = _inproj(xn, w_t, cos, sin, cw, cb, _MAIN, "inproj_main")
    dil_a, dil_b = _inproj(xn, w_t, cos, sin, cw, cb, _DILATED, "inproj_dilated")
    gates, dt, dt_t = _gate_proj(xn, w_t, gate_b[0].reshape(1, 2 * D_MODEL), dt_bias[0])
    attn = _attention(p_main, dil_a, dil_b)
    ssd = _ssd(p_main, dt, dt_t, a_log[0], d_skip[0], ssd_norm_w[0])
    merged = _branches(attn, ssd, w_attn_br, w_ssd_br, gates)
    out = _out_proj(merged, w_out[0].astype(BF16), x2, final_norm_w)
    return out.reshape(BATCH, SEQ, D_MODEL)
```

```python
import functools

import jax
import jax.numpy as jnp
from jax import lax
from jax.experimental import pallas as pl
from jax.experimental.pallas import tpu as pltpu

F32 = jnp.float32
BF16 = jnp.bfloat16
HIGHEST = lax.Precision.HIGHEST

D_MODEL = 2048
BATCH = 4
SEQ = 2048
TOKENS = BATCH * SEQ
NORM_EPS = 1e-5

HEAD_DIM = 128
HEADS_PER_GROUP = 8
GROUP_WIDTH = HEADS_PER_GROUP * HEAD_DIM
ATTN_BLOCK = 128
DILATIONS = (4, 16)
ROPE_DIMS = HEAD_DIM // 4
ROPE_HALF = ROPE_DIMS // 2
ROPE_THETA = 500000.0
Q_SCALE = HEAD_DIM ** -0.5
LOG2_E = 1.4426950408889634

SSD_INNER = 2 * D_MODEL
SSD_HEAD_DIM = 64
SSD_HEADS = SSD_INNER // SSD_HEAD_DIM
SSD_GROUPS = 8
SSD_HEADS_PER_GROUP = SSD_HEADS // SSD_GROUPS
SSD_STATE = 128
SSD_CONV = 4
SSD_CHUNK = 128
SSD_GROUP_WIDTH = SSD_HEADS_PER_GROUP * SSD_HEAD_DIM
SSD_BC_WIDTH = SSD_GROUPS * SSD_STATE
SSD_CONV_CH = SSD_INNER + 2 * SSD_BC_WIDTH
N_CHUNKS = SEQ // SSD_CHUNK

COL_DT = 3 * 3 * GROUP_WIDTH + GROUP_WIDTH + SSD_INNER + SSD_CONV_CH
COL_GATES = COL_DT + SSD_HEADS

LANES = 128
ROW_TILE = 512
PROJ_ROWS = 1024
COL_TILE = 1024
HALF_TILE = COL_TILE // 2
SUB_COLS = 256
SUBS_PER_HALF = HALF_TILE // SUB_COLS
DEINT_STRIDE = 4
VMEM_LIMIT = 56 * 1024 * 1024


NT_DIMS = (((1,), (1,)), ((), ()))


def _params(*sem, flags=None):
    return pltpu.CompilerParams(dimension_semantics=sem, vmem_limit_bytes=VMEM_LIMIT, flags=flags)


def _rmsnorm_kernel(x_ref, w_ref, o_ref):
    x = x_ref[...]
    ms = jnp.mean(x * x, axis=-1, keepdims=True)
    o_ref[...] = (x * lax.rsqrt(ms + NORM_EPS) * w_ref[...]).astype(o_ref.dtype)


def _rmsnorm(x2, w):
    return pl.pallas_call(
        _rmsnorm_kernel,
        grid=(TOKENS // ROW_TILE,),
        in_specs=[pl.BlockSpec((ROW_TILE, D_MODEL), lambda i: (i, 0)),
                  pl.BlockSpec((1, D_MODEL), lambda i: (0, 0))],
        out_specs=pl.BlockSpec((ROW_TILE, D_MODEL), lambda i: (i, 0)),
        out_shape=jax.ShapeDtypeStruct((TOKENS, D_MODEL), BF16),
        compiler_params=_params("arbitrary"),
        name="rmsnorm",
    )(x2, w.reshape(1, D_MODEL))


def _rope_table_kernel(pos_ref, invf_ref, cos_ref, sin_ref):
    ang = pos_ref[...].astype(F32) * invf_ref[...]
    lane = lax.broadcasted_iota(jnp.int32, ang.shape, 1)
    sign = jnp.where(lane < ROPE_HALF, -1.0, jnp.where(lane < ROPE_DIMS, 1.0, 0.0))
    cos_ref[...] = jnp.cos(ang)
    sin_ref[...] = jnp.sin(ang) * sign


def _rope_tables(positions):
    inv_freq = ROPE_THETA ** (-jnp.arange(ROPE_HALF, dtype=F32) / ROPE_HALF)
    invf = jnp.zeros((1, LANES), F32).at[0, :ROPE_DIMS].set(jnp.tile(inv_freq, 2))
    rows = 1024
    return pl.pallas_call(
        _rope_table_kernel,
        grid=(TOKENS // rows,),
        in_specs=[pl.BlockSpec((rows, 1), lambda i: (i, 0)),
                  pl.BlockSpec((1, LANES), lambda i: (0, 0))],
        out_specs=[pl.BlockSpec((rows, LANES), lambda i: (i, 0))] * 2,
        out_shape=[jax.ShapeDtypeStruct((TOKENS, LANES), F32)] * 2,
        compiler_params=_params("arbitrary"),
        name="rope_tables",
    )(positions.reshape(TOKENS, 1), invf)


def _silu(v):
    return 0.5 * v * (1.0 + jnp.tanh(0.5 * v))


def _lookup(n, values):
    out = values[-1]
    for k in range(len(values) - 2, -1, -1):
        out = jnp.where(n == k, values[k], out)
    return out


def _inproj_kernel(x_ref, wa_ref, wb_ref, cos_ref, sin_ref, cw_ref, cb_ref, *refs, plan, split_out):
    if split_out:
        oa_ref, ob_ref, w_s, stage_a, stage_b, deint_ref = refs
    else:
        o_ref, w_s, stage_a, stage_b, deint_ref = refs
        oa_ref = o_ref.at[:, 0:HALF_TILE]
        ob_ref = o_ref.at[:, HALF_TILE:COL_TILE]
    n = pl.program_id(0)
    i = pl.program_id(1)

    @pl.when(i == 0)
    def _():
        w_s[0:HALF_TILE, :] = wa_ref[...].astype(BF16)
        w_s[HALF_TILE:COL_TILE, :] = wb_ref[...].astype(BF16)

    def sub_dot(c):
        return lax.dot_general(x_ref[...], w_s[c * SUB_COLS:(c + 1) * SUB_COLS, :], NT_DIMS,
                               preferred_element_type=F32)

    stages = (stage_a, stage_b)
    top = stage_a.shape[1] - PROJ_ROWS
    body = slice(top, top + PROJ_ROWS)
    slot0 = lax.shift_right_logical(i, 30)

    def emit(val, o_ref, col, c, h, dil):
        if dil == 1:
            o_ref[:, col:col + LANES] = val.astype(o_ref.dtype)
            return
        slot = c * (SUB_COLS // LANES) + h
        slab = deint_ref.at[0, slot]
        slab[...] = val
        groups = 1
        if dil != DEINT_STRIDE:
            assert dil == DEINT_STRIDE ** 2
            by_low = deint_ref.at[1, slot]
            cnt = PROJ_ROWS // DEINT_STRIDE
            for low in range(DEINT_STRIDE):
                by_low[low * cnt:(low + 1) * cnt, :] = slab[pl.ds(low, cnt, stride=DEINT_STRIDE), :]
            slab, groups = by_low, DEINT_STRIDE
        per_group = PROJ_ROWS // groups
        cnt = PROJ_ROWS // dil
        for low in range(groups):
            for high in range(DEINT_STRIDE):
                r = high * groups + low
                o_ref[r * cnt:(r + 1) * cnt, col:col + LANES] = (
                    slab[pl.ds(low * per_group + high, cnt, stride=DEINT_STRIDE), :].astype(o_ref.dtype))

    def epilogue(kind, c, o_ref):
        kind, dil = kind if isinstance(kind, tuple) else (kind, 1)
        lc = c % SUBS_PER_HALF
        cols = slice(lc * SUB_COLS, (lc + 1) * SUB_COLS)
        if kind == "plain" and dil > 1:
            acc = sub_dot(c)
            for h in range(SUB_COLS // LANES):
                emit(acc[:, h * LANES:(h + 1) * LANES], o_ref, lc * SUB_COLS + h * LANES, c, h, dil)
        elif kind == "plain":
            o_ref[:, cols] = sub_dot(c).astype(o_ref.dtype)
        elif kind == "silu":
            o_ref[:, cols] = _silu(sub_dot(c)).astype(o_ref.dtype)
        elif kind == "conv":
            first = (i % (SEQ // PROJ_ROWS)) == 0
            st = stages[c % 2].at[c // 2]
            last = st[PROJ_ROWS:PROJ_ROWS + top, :]
            st[0:top, :] = jnp.where(first, 0.0, last)
            st[body, :] = sub_dot(c)
            rd = stages[c % 2].at[slot0 + c // 2]
            wcols = slice(c * SUB_COLS, (c + 1) * SUB_COLS)
            w = cw_ref[:, wcols]
            y = rd[body, :] * w[SSD_CONV - 1:SSD_CONV, :] + cb_ref[:, wcols]
            for s in range(1, SSD_CONV):
                y = y + rd[top - s:top - s + PROJ_ROWS, :] * w[SSD_CONV - 1 - s:SSD_CONV - s, :]
            o_ref[:, cols] = _silu(y).astype(o_ref.dtype)
        else:
            acc = sub_dot(c)
            cos = cos_ref[...]
            sin = sin_ref[...]
            scale = Q_SCALE * LOG2_E if kind == "rope_q" else 1.0
            lane = lax.broadcasted_iota(jnp.int32, (PROJ_ROWS, LANES), 1)
            for h in range(SUB_COLS // HEAD_DIM):
                t = acc[:, h * HEAD_DIM:(h + 1) * HEAD_DIM]
                partner = jnp.where(lane < ROPE_HALF,
                                    pltpu.roll(t, LANES - ROPE_HALF, 1),
                                    pltpu.roll(t, ROPE_HALF, 1))
                emit((t * cos + partner * sin) * scale, o_ref, lc * SUB_COLS + h * HEAD_DIM, c, h, dil)

    for lo, hi, kind_a, kind_b in plan:
        @pl.when((n >= lo) & (n < hi))
        def _(kind_a=kind_a, kind_b=kind_b):
            for c in range(SUBS_PER_HALF):
                epilogue(kind_a, c, oa_ref)
            for c in range(SUBS_PER_HALF, 2 * SUBS_PER_HALF):
                epilogue(kind_b, c, ob_ref)


_W_Q, _W_K, _W_V = 0, 6, 12
_W_ZATTN, _W_ZSSD, _W_XBC = 18, 20, 28
_MAIN = dict(
    wa=[_W_Q, _W_Q + 1, _W_K, _W_K + 1] + [_W_ZSSD + 2 * k for k in range(4)]
    + [_W_XBC + 2 * k for k in range(6)],
    wb=[_W_V, _W_V + 1, _W_ZATTN, _W_ZATTN + 1] + [_W_ZSSD + 2 * k + 1 for k in range(4)]
    + [_W_XBC + 2 * k + 1 for k in range(6)],
    plan=((0, 2, "rope_q", "plain"), (2, 4, "rope_k", "silu"), (4, 8, "silu", "silu"),
          (8, 14, "conv", "conv")),
    conv_first=8, split_out=False, dtype=BF16)
_DILATED = dict(
    wa=[_W_Q + 2, _W_Q + 3, _W_Q + 4, _W_Q + 5, _W_K + 2, _W_K + 3],
    wb=[_W_V + 2, _W_V + 3, _W_V + 4, _W_V + 5, _W_K + 4, _W_K + 5],
    plan=((0, 2, ("rope_q", 4), ("plain", 4)), (2, 4, ("rope_q", 16), ("plain", 16)),
          (4, 6, ("rope_k", 4), ("rope_k", 16))),
    conv_first=0, split_out=True, dtype=BF16)


def _inproj(xn, w_t, cos, sin, conv_w, conv_b, cfg, name):
    steps = len(cfg["wa"])
    t_spec = pl.BlockSpec((PROJ_ROWS, LANES), lambda n, i: (i, 0))
    conv_col = lambda n, i: (0, jnp.clip(n - cfg["conv_first"], 0, SSD_CONV_CH // COL_TILE - 1))
    w_spec = lambda rows: pl.BlockSpec((None, HALF_TILE, D_MODEL), lambda n, i: (0, _lookup(n, rows), 0))
    if cfg["split_out"]:
        out_specs = [pl.BlockSpec((PROJ_ROWS, HALF_TILE), lambda n, i: (i, n))] * 2
        out_shape = [jax.ShapeDtypeStruct((TOKENS, steps * HALF_TILE), cfg["dtype"])] * 2
    else:
        out_specs = pl.BlockSpec((PROJ_ROWS, COL_TILE), lambda n, i: (i, n))
        out_shape = jax.ShapeDtypeStruct((TOKENS, steps * COL_TILE), cfg["dtype"])
    stage = pltpu.VMEM((2, 8 + PROJ_ROWS, SUB_COLS), F32)
    return pl.pallas_call(
        functools.partial(_inproj_kernel, plan=cfg["plan"], split_out=cfg["split_out"]),
        grid=(steps, TOKENS // PROJ_ROWS),
        in_specs=[pl.BlockSpec((PROJ_ROWS, D_MODEL), lambda n, i: (i, 0)),
                  w_spec(cfg["wa"]), w_spec(cfg["wb"]), t_spec, t_spec,
                  pl.BlockSpec((SSD_CONV, COL_TILE), conv_col),
                  pl.BlockSpec((1, COL_TILE), conv_col)],
        out_specs=out_specs,
        out_shape=out_shape,
        scratch_shapes=[pltpu.VMEM((COL_TILE, D_MODEL), BF16), stage, stage,
                        pltpu.VMEM((2, COL_TILE // LANES, PROJ_ROWS, LANES), F32)],
        compiler_params=_params("arbitrary", "arbitrary"),
        name=name,
    )(xn, w_t, w_t, cos, sin, conv_w, conv_b)


def _gate_kernel(x_ref, wa_ref, wb_ref, b_ref, dtb_col_ref, o_ref, dt_ref, dtt_ref, w_s):
    @pl.when(pl.program_id(1) == 0)
    def _():
        w_s[0:COL_TILE - SSD_HEADS, :] = wa_ref[SSD_HEADS:COL_TILE, :].astype(BF16)
        w_s[COL_TILE - SSD_HEADS:COL_TILE, :] = wb_ref[...].astype(BF16)

    @pl.when(pl.program_id(0) == 0)
    def _():
        w = wa_ref[0:SSD_HEADS, :].astype(BF16)
        dtt = jax.nn.softplus(
            lax.dot_general(w, x_ref[...], NT_DIMS, preferred_element_type=F32) + dtb_col_ref[...])
        dtt_ref[...] = dtt
        dt_ref[...] = dtt.T

    for c in range(COL_TILE // SUB_COLS):
        cols = slice(c * SUB_COLS, (c + 1) * SUB_COLS)
        acc = lax.dot_general(x_ref[...], w_s[cols, :], NT_DIMS, preferred_element_type=F32)
        logits = acc + b_ref[:, cols]
        o_ref[:, cols] = (0.5 + 0.5 * jnp.tanh(0.5 * logits)).astype(o_ref.dtype)


def _gate_proj(xn, w_t, gate_bias, dt_bias):
    n_out = 2 * D_MODEL
    first = COL_DT // COL_TILE
    per = COL_TILE // SSD_HEADS
    n_rows = TOKENS // PROJ_ROWS
    dt_row = lambda n, i: jnp.where(n == 0, i, n_rows - 1)
    return pl.pallas_call(
        _gate_kernel,
        grid=(n_out // COL_TILE, n_rows),
        in_specs=[pl.BlockSpec((PROJ_ROWS, D_MODEL), lambda n, i: (i, 0)),
                  pl.BlockSpec((None, COL_TILE, D_MODEL), lambda n, i: (0, first + n, 0)),
                  pl.BlockSpec((None, SSD_HEADS, D_MODEL), lambda n, i: (0, (first + n + 1) * per, 0)),
                  pl.BlockSpec((1, COL_TILE), lambda n, i: (0, n)),
                  pl.BlockSpec((SSD_HEADS, 1), lambda n, i: (0, 0))],
        out_specs=[pl.BlockSpec((PROJ_ROWS, COL_TILE), lambda n, i: (i, n)),
                   pl.BlockSpec((PROJ_ROWS, SSD_HEADS), lambda n, i: (dt_row(n, i), 0)),
                   pl.BlockSpec((SSD_HEADS, PROJ_ROWS), lambda n, i: (0, dt_row(n, i)))],
        out_shape=[jax.ShapeDtypeStruct((TOKENS, n_out), BF16),
                   jax.ShapeDtypeStruct((TOKENS, SSD_HEADS), F32),
                   jax.ShapeDtypeStruct((SSD_HEADS, TOKENS), F32)],
        scratch_shapes=[pltpu.VMEM((COL_TILE, D_MODEL), BF16)],
        compiler_params=_params("arbitrary", "arbitrary"),
        name="gate_proj",
    )(xn, w_t, w_t, gate_bias, dt_bias.reshape(SSD_HEADS, 1))


def _attn_kernel(q0, k0, v0, q1, k1, v1, q2, k2, v2, z_ref, o_ref, m_s, a_s, d_s, *perm):
    blk = ATTN_BLOCK
    qi = lax.broadcasted_iota(jnp.int32, (blk, blk), 0)
    ki = lax.broadcasted_iota(jnp.int32, (blk, blk), 1)
    mask_cur = ki <= qi
    mask_prev = ki >= qi
    nb = SEQ // blk
    qk_dims = (((2,), (2,)), ((0,), (0,)))
    pv_dims = (((2,), (1,)), ((0,), (0,)))

    for src, dst, dil in zip((q1, k1, v1, q2, k2, v2), perm, (DILATIONS[0],) * 3 + (DILATIONS[1],) * 3):
        sub = SEQ // dil
        cnt = PROJ_ROWS // dil
        for r in range(dil):
            for blk_i in range(SEQ // PROJ_ROWS):
                dst[r * sub + blk_i * cnt:r * sub + (blk_i + 1) * cnt, :] = (
                    src[blk_i * PROJ_ROWS + r * cnt:blk_i * PROJ_ROWS + (r + 1) * cnt, :])

    def group(q_ref, k_ref, v_ref, blocks_per_seq):
        q = q_ref[...].reshape(nb, blk, HEAD_DIM)
        k = k_ref[...].reshape(nb, blk, HEAD_DIM)
        v = jnp.concatenate([v_ref[...], jnp.ones((SEQ, LANES), BF16)], axis=1).reshape(nb, blk, -1)
        s_cur = lax.dot_general(q, k, qk_dims, preferred_element_type=F32)
        s_cur = jnp.where(mask_cur[None], s_cur, -jnp.inf)
        m = jnp.max(s_cur, axis=-1, keepdims=True)
        if blocks_per_seq > 1:
            k_prev = jnp.concatenate([k[:1], k[:-1]], axis=0)
            v_prev = jnp.concatenate([v[:1], v[:-1]], axis=0)
            b_idx = lax.broadcasted_iota(jnp.int32, (nb, 1, 1), 0)
            no_prev = jnp.where(b_idx % blocks_per_seq == 0, -jnp.inf, 0.0)
            s_prev = lax.dot_general(q, k_prev, qk_dims, preferred_element_type=F32)
            s_prev = jnp.where(mask_prev[None], s_prev, -jnp.inf) + no_prev
            m = jnp.maximum(m, jnp.max(s_prev, axis=-1, keepdims=True))
        p = jnp.exp2(s_cur - m)
        acc = lax.dot_general(p.astype(BF16), v, pv_dims, preferred_element_type=F32)
        if blocks_per_seq > 1:
            p = jnp.exp2(s_prev - m)
            acc = acc + lax.dot_general(p.astype(BF16), v_prev, pv_dims, preferred_element_type=F32)
        acc = acc.reshape(SEQ, HEAD_DIM + LANES)
        m = jnp.broadcast_to(m, (nb, blk, LANES)).reshape(SEQ, LANES)
        return m, acc[:, HEAD_DIM:], acc[:, :HEAD_DIM]

    for buf, val in zip((m_s, d_s, a_s), group(q0, k0, v0, nb)):
        buf[0] = val
    for g, dil in enumerate(DILATIONS):
        sub = SEQ // dil
        for buf, val in zip((m_s, d_s, a_s), group(*perm[3 * g:3 * g + 3], sub // blk)):
            for r in range(dil):
                buf[g + 1, pl.ds(r, sub, stride=dil), :] = val[r * sub:(r + 1) * sub, :]

    n_groups = len(DILATIONS) + 1
    m_all = [m_s[g] for g in range(n_groups)]
    m_max = functools.reduce(jnp.maximum, m_all)
    w = [jnp.exp2(m - m_max) for m in m_all]
    num = sum(w[g] * a_s[g] for g in range(n_groups))
    den = sum(w[g] * d_s[g] for g in range(n_groups))
    o_ref[...] = (num / den * z_ref[...].astype(F32)).astype(o_ref.dtype)


def _attention(p_main, dil_a, dil_b):
    hb = HEADS_PER_GROUP

    def spec(sec):
        return pl.BlockSpec((SEQ, HEAD_DIM), lambda b, h: (b, sec * hb + h))

    def main_spec(first):
        per = HALF_TILE // HEAD_DIM
        return pl.BlockSpec((SEQ, HEAD_DIM), lambda b, h: (b, first * per + h + per * (h // per)))

    return pl.pallas_call(
        _attn_kernel,
        grid=(BATCH, hb),
        in_specs=[main_spec(0), main_spec(4), main_spec(1), spec(0), spec(2), spec(0), spec(1), spec(2),
                  spec(1), main_spec(5)],
        out_specs=pl.BlockSpec((SEQ, HEAD_DIM), lambda b, h: (b, h)),
        out_shape=jax.ShapeDtypeStruct((TOKENS, GROUP_WIDTH), BF16),
        scratch_shapes=([pltpu.VMEM((len(DILATIONS) + 1, SEQ, LANES), F32)] * 3
                        + [pltpu.VMEM((SEQ, HEAD_DIM), BF16)] * 6),
        compiler_params=_params("arbitrary", "arbitrary"),
        name="dilated_attn",
    )(p_main, p_main, p_main, dil_a, dil_a, dil_b, dil_a, dil_b, dil_b, p_main)


def _ssd_kernel(xs_ref, b_ref, c_ref, z_ref, dt_ref, dtt_ref, arow_ref, acol_ref, dskip_ref, nw_ref,
                o_ref, state, y_s, cs3, cst_s, wrow_s, cdec_s):
    L = SSD_CHUNK
    G = SSD_GROUPS
    J = SSD_HEADS_PER_GROUP
    GW = SSD_GROUP_WIDTH
    N = SSD_STATE

    @pl.when(pl.program_id(1) == 0)
    def _():
        state[...] = jnp.zeros_like(state)

    li = lax.broadcasted_iota(jnp.int32, (L, L), 0)
    si = lax.broadcasted_iota(jnp.int32, (L, L), 1)
    tri = li >= si
    a_row = -jnp.exp(arow_ref[...])
    a_col = -jnp.exp(acol_ref[...])
    dtt = dtt_ref[...]
    cs = LOG2_E * jnp.dot(tri.astype(F32), dt_ref[...] * a_row, precision=HIGHEST,
                          preferred_element_type=F32)
    cst = LOG2_E * jnp.dot(dtt * a_col, (si >= li).astype(F32), precision=HIGHEST,
                           preferred_element_type=F32)
    cs_last = cst[:, L - 1:L]
    cst_s[...] = cst - jnp.log2(dtt)
    wrow_s[...] = dtt * jnp.exp2(cs_last - cst)
    cdec_s[...] = jnp.broadcast_to(jnp.exp2(cs_last), (SSD_HEADS, LANES))
    for g in range(G):
        cs3[g] = cs[:, g * J:(g + 1) * J]

    lane = lax.broadcasted_iota(jnp.int32, (L, LANES), 1)
    lo_half = lane < SSD_HEAD_DIM

    def split(v):
        zero = jnp.zeros_like(v)
        return jnp.where(lo_half, v, zero), jnp.where(lo_half, zero, v)

    def group(g):
        cols = slice(g * GW, (g + 1) * GW)
        ncols = slice(g * N, (g + 1) * N)
        heads = slice(g * J, (g + 1) * J)
        x_g = xs_ref[:, cols]
        b_g = b_ref[:, ncols]
        c_g = c_ref[:, ncols]
        cb = lax.dot_general(c_g, b_g, NT_DIMS, preferred_element_type=F32)
        bt = b_g.astype(F32).T
        cf = c_g.astype(F32)
        cs_g = cs3[g]
        cst_g = cst_s[heads, :]
        w_g = wrow_s[heads, :]
        cd_g = cdec_s[heads, :]
        st_prev = state[g]
        y_parts = []
        s_parts = []
        for p in range(J // 2):
            lhs_y = []
            lhs_s = []
            for j in (2 * p, 2 * p + 1):
                col = jnp.broadcast_to(cs_g[:, j:j + 1], (L, LANES))
                decay = jnp.exp2(jnp.where(tri, col - cst_g[j:j + 1, :], -jnp.inf))
                lhs_y.append((cb * decay).astype(BF16))
                lhs_y.append((cf * jnp.exp2(col)).astype(BF16))
                lhs_s.append((bt * w_g[j:j + 1, :]).astype(BF16))
            lanes_p = slice(p * LANES, (p + 1) * LANES)
            x_lo, x_hi = split(x_g[:, lanes_p])
            s_p = st_prev[:, lanes_p]
            s_lo, s_hi = split(s_p.astype(BF16))
            y_parts.append(jnp.dot(jnp.concatenate(lhs_y, axis=1),
                                   jnp.concatenate([x_lo, s_lo, x_hi, s_hi], axis=0),
                                   preferred_element_type=F32))
            cd = jnp.where(lo_half[0:1, :], cd_g[2 * p:2 * p + 1, :], cd_g[2 * p + 1:2 * p + 2, :])
            s_parts.append(s_p * cd + jnp.dot(jnp.concatenate(lhs_s, axis=1),
                                              jnp.concatenate([x_lo, x_hi], axis=0),
                                              preferred_element_type=F32))
        state[g] = jnp.concatenate(s_parts, axis=1)
        y = jnp.concatenate(y_parts, axis=1) + x_g.astype(F32) * dskip_ref[:, cols]
        y_s[:, cols] = y * z_ref[:, cols].astype(F32)

    for g in range(G):
        group(g)

    y = y_s[...]
    ms = jnp.mean(y * y, axis=-1, keepdims=True)
    o_ref[...] = (y * lax.rsqrt(ms + NORM_EPS) * nw_ref[...]).astype(o_ref.dtype)


def _ssd(p_main, dt, dt_t, a_log, d_skip, norm_w):
    L = SSD_CHUNK
    row = lambda b, c: b * N_CHUNKS + c
    const = lambda b, c: (0, 0)
    return pl.pallas_call(
        _ssd_kernel,
        grid=(BATCH, N_CHUNKS),
        in_specs=[
            pl.BlockSpec((L, SSD_INNER), lambda b, c: (row(b, c), 2)),
            pl.BlockSpec((L, SSD_BC_WIDTH), lambda b, c: (row(b, c), 12)),
            pl.BlockSpec((L, SSD_BC_WIDTH), lambda b, c: (row(b, c), 13)),
            pl.BlockSpec((L, SSD_INNER), lambda b, c: (row(b, c), 1)),
            pl.BlockSpec((L, SSD_HEADS), lambda b, c: (row(b, c), 0)),
            pl.BlockSpec((SSD_HEADS, L), lambda b, c: (0, row(b, c))),
            pl.BlockSpec((1, SSD_HEADS), const),
            pl.BlockSpec((SSD_HEADS, 1), const),
            pl.BlockSpec((1, SSD_INNER), const),
            pl.BlockSpec((1, SSD_INNER), const),
        ],
        out_specs=pl.BlockSpec((L, SSD_INNER), lambda b, c: (row(b, c), 0)),
        out_shape=jax.ShapeDtypeStruct((TOKENS, SSD_INNER), BF16),
        scratch_shapes=[
            pltpu.VMEM((SSD_GROUPS, SSD_STATE, SSD_GROUP_WIDTH), F32),
            pltpu.VMEM((L, SSD_INNER), F32),
            pltpu.VMEM((SSD_GROUPS, L, SSD_HEADS_PER_GROUP), F32),
            pltpu.VMEM((SSD_HEADS, L), F32),
            pltpu.VMEM((SSD_HEADS, L), F32),
            pltpu.VMEM((SSD_HEADS, LANES), F32),
        ],
        compiler_params=_params("arbitrary", "arbitrary"),
        name="ssd",
    )(p_main, p_main, p_main, p_main, dt, dt_t,
      a_log.reshape(1, SSD_HEADS), a_log.reshape(SSD_HEADS, 1),
      jnp.repeat(d_skip, SSD_HEAD_DIM).reshape(1, SSD_INNER), norm_w.reshape(1, SSD_INNER))


def _branch_kernel(a_ref, s_ref, wa_ref, ws_ref, g0_ref, g1_ref, o_ref, wab, wsb):
    @pl.when(pl.program_id(1) == 0)
    def _():
        wab[...] = wa_ref[...].astype(BF16)
        wsb[...] = ws_ref[...].astype(BF16)

    for c in range(o_ref.shape[1] // SUB_COLS):
        cols = slice(c * SUB_COLS, (c + 1) * SUB_COLS)
        ya = jnp.dot(a_ref[...], wab[:, cols], preferred_element_type=F32)
        ys = jnp.dot(s_ref[...], wsb[:, cols], preferred_element_type=F32)
        merged = g0_ref[:, cols].astype(F32) * ya + g1_ref[:, cols].astype(F32) * ys
        o_ref[:, cols] = merged.astype(o_ref.dtype)


def _branches(attn, ssd, w_attn, w_ssd, gates):
    tn = 512
    nj = D_MODEL // tn
    return pl.pallas_call(
        _branch_kernel,
        grid=(nj, TOKENS // PROJ_ROWS),
        in_specs=[pl.BlockSpec((PROJ_ROWS, GROUP_WIDTH), lambda n, i: (i, 0)),
                  pl.BlockSpec((PROJ_ROWS, SSD_INNER), lambda n, i: (i, 0)),
                  pl.BlockSpec((None, GROUP_WIDTH, tn), lambda n, i: (0, 0, n)),
                  pl.BlockSpec((None, SSD_INNER, tn), lambda n, i: (0, 0, n)),
                  pl.BlockSpec((PROJ_ROWS, tn), lambda n, i: (i, n)),
                  pl.BlockSpec((PROJ_ROWS, tn), lambda n, i: (i, nj + n))],
        out_specs=pl.BlockSpec((PROJ_ROWS, tn), lambda n, i: (i, n)),
        out_shape=jax.ShapeDtypeStruct((TOKENS, D_MODEL), BF16),
        scratch_shapes=[pltpu.VMEM((GROUP_WIDTH, tn), BF16), pltpu.VMEM((SSD_INNER, tn), BF16)],
        compiler_params=_params("arbitrary", "arbitrary"),
        name="branches",
    )(attn, ssd, w_attn, w_ssd, gates, gates)


def _out_kernel(m_ref, w_ref, x_ref, fw_ref, o_ref):
    sub = 2 * SUB_COLS
    ssq = jnp.zeros((o_ref.shape[0], 1), F32)
    for c in range(D_MODEL // sub):
        cols = slice(c * sub, (c + 1) * sub)
        h = x_ref[:, cols] + jnp.dot(m_ref[...], w_ref[:, cols], preferred_element_type=F32)
        ssq = ssq + jnp.sum(h * h, axis=-1, keepdims=True)
        o_ref[:, cols] = h
    o_ref[...] = o_ref[...] * lax.rsqrt(ssq / D_MODEL + NORM_EPS) * fw_ref[...]


def _out_proj(merged, w_out_bf, x2, final_w):
    tm = ROW_TILE
    return pl.pallas_call(
        _out_kernel,
        grid=(TOKENS // tm,),
        in_specs=[pl.BlockSpec((tm, D_MODEL), lambda i: (i, 0)),
                  pl.BlockSpec((D_MODEL, D_MODEL), lambda i: (0, 0)),
                  pl.BlockSpec((tm, D_MODEL), lambda i: (i, 0)),
                  pl.BlockSpec((1, D_MODEL), lambda i: (0, 0))],
        out_specs=pl.BlockSpec((tm, D_MODEL), lambda i: (i, 0)),
        out_shape=jax.ShapeDtypeStruct((TOKENS, D_MODEL), F32),
        compiler_params=_params("arbitrary"),
        name="out_proj",
    )(merged, w_out_bf, x2, final_w.reshape(1, D_MODEL))


def kernel(x, positions, norm_w, w_in, conv_w, conv_b, dt_bias, a_log, d_skip, ssd_norm_w,
           w_attn_br, w_ssd_br, gate_b, w_out, final_norm_w):
    assert x.shape == (BATCH, SEQ, D_MODEL) and w_in.shape[0] == 1
    x2 = x.reshape(TOKENS, D_MODEL)
    xn = _rmsnorm(x2, norm_w[0])
    cos, sin = _rope_tables(positions)
    w_t = jnp.swapaxes(w_in, 1, 2)
    cw, cb = conv_w[0], conv_b.reshape(1, SSD_CONV_CH)
    p_main = _inproj(xn, w_t, cos, sin, cw, cb, _MAIN, "inproj_main")
    dil_a, dil_b = _inproj(xn, w_t, cos, sin, cw, cb, _DILATED, "inproj_dilated")
    gates, dt, dt_t = _gate_proj(xn, w_t, gate_b[0].reshape(1, 2 * D_MODEL), dt_bias[0])
    attn = _attention(p_main, dil_a, dil_b)
    ssd = _ssd(p_main, dt, dt_t, a_log[0], d_skip[0], ssd_norm_w[0])
    merged = _branches(attn, ssd, w_attn_br, w_ssd_br, gates)
    out = _out_proj(merged, w_out[0].astype(BF16), x2, final_norm_w)
    return out.reshape(BATCH, SEQ, D_MODEL)
```

```python
import functools

import jax
import jax.numpy as jnp
from jax import lax
from jax.experimental import pallas as pl
from jax.experimental.pallas import tpu as pltpu

F32 = jnp.float32
BF16 = jnp.bfloat16
HIGHEST = lax.Precision.HIGHEST

D_MODEL = 2048
BATCH = 4
SEQ = 2048
TOKENS = BATCH * SEQ
NORM_EPS = 1e-5

HEAD_DIM = 128
HEADS_PER_GROUP = 8
GROUP_WIDTH = HEADS_PER_GROUP * HEAD_DIM
ATTN_BLOCK = 128
DILATIONS = (4, 16)
ROPE_DIMS = HEAD_DIM // 4
ROPE_HALF = ROPE_DIMS // 2
ROPE_THETA = 500000.0
Q_SCALE = HEAD_DIM ** -0.5
LOG2_E = 1.4426950408889634

SSD_INNER = 2 * D_MODEL
SSD_HEAD_DIM = 64
SSD_HEADS = SSD_INNER // SSD_HEAD_DIM
SSD_GROUPS = 8
SSD_HEADS_PER_GROUP = SSD_HEADS // SSD_GROUPS
SSD_STATE = 128
SSD_CONV = 4
SSD_CHUNK = 128
SSD_GROUP_WIDTH = SSD_HEADS_PER_GROUP * SSD_HEAD_DIM
SSD_BC_WIDTH = SSD_GROUPS * SSD_STATE
SSD_CONV_CH = SSD_INNER + 2 * SSD_BC_WIDTH
N_CHUNKS = SEQ // SSD_CHUNK

COL_DT = 3 * 3 * GROUP_WIDTH + GROUP_WIDTH + SSD_INNER + SSD_CONV_CH
COL_GATES = COL_DT + SSD_HEADS

LANES = 128
ROW_TILE = 512
PROJ_ROWS = 1024
COL_TILE = 1024
HALF_TILE = COL_TILE // 2
SUB_COLS = 256
SUBS_PER_HALF = HALF_TILE // SUB_COLS
ROW_SPLIT = 2
DEINT_STRIDE = 4
VMEM_LIMIT = 56 * 1024 * 1024


NT_DIMS = (((1,), (1,)), ((), ()))


def _params(*sem):
    return pltpu.CompilerParams(dimension_semantics=sem, vmem_limit_bytes=VMEM_LIMIT)


def _prep_kernel(x_ref, w_ref, pos_ref, invf_ref, xn_ref, cos_ref, sin_ref):
    x = x_ref[...]
    ms = jnp.mean(x * x, axis=-1, keepdims=True)
    xn_ref[...] = (x * lax.rsqrt(ms + NORM_EPS) * w_ref[...]).astype(xn_ref.dtype)
    ang = pos_ref[...].astype(F32) * invf_ref[...]
    lane = lax.broadcasted_iota(jnp.int32, ang.shape, 1)
    sign = jnp.where(lane < ROPE_HALF, -1.0, jnp.where(lane < ROPE_DIMS, 1.0, 0.0))
    cos_ref[...] = jnp.cos(ang)
    sin_ref[...] = jnp.sin(ang) * sign


def _prep(x2, w, positions):
    inv_freq = ROPE_THETA ** (-jnp.arange(ROPE_HALF, dtype=F32) / ROPE_HALF)
    invf = jnp.zeros((1, LANES), F32).at[0, :ROPE_DIMS].set(jnp.tile(inv_freq, 2))
    row_spec = lambda width: pl.BlockSpec((ROW_TILE, width), lambda i: (i, 0))
    const_spec = lambda width: pl.BlockSpec((1, width), lambda i: (0, 0))
    return pl.pallas_call(
        _prep_kernel,
        grid=(TOKENS // ROW_TILE,),
        in_specs=[row_spec(D_MODEL), const_spec(D_MODEL), row_spec(1), const_spec(LANES)],
        out_specs=[row_spec(D_MODEL), row_spec(LANES), row_spec(LANES)],
        out_shape=[jax.ShapeDtypeStruct((TOKENS, D_MODEL), BF16),
                   jax.ShapeDtypeStruct((TOKENS, LANES), F32),
                   jax.ShapeDtypeStruct((TOKENS, LANES), F32)],
        compiler_params=_params("arbitrary"),
        name="prep",
    )(x2, w.reshape(1, D_MODEL), positions.reshape(TOKENS, 1), invf)


def _silu(v):
    return 0.5 * v * (1.0 + jnp.tanh(0.5 * v))


def _lookup(n, values):
    out = values[-1]
    for k in range(len(values) - 2, -1, -1):
        out = jnp.where(n == k, values[k], out)
    return out


def _inproj_kernel(x_ref, wa_ref, wb_ref, cos_ref, sin_ref, cw_ref, cb_ref, *refs, plan, split_out):
    if split_out:
        oa_ref, ob_ref, w_s, deint_ref, *stages = refs
    else:
        o_ref, w_s, deint_ref, *stages = refs
        oa_ref = o_ref.at[:, 0:HALF_TILE]
        ob_ref = o_ref.at[:, HALF_TILE:COL_TILE]
    n = pl.program_id(0)
    i = pl.program_id(1)

    @pl.when(i == 0)
    def _():
        w_s[0:HALF_TILE, :] = wa_ref[...].astype(BF16)
        w_s[HALF_TILE:COL_TILE, :] = wb_ref[...].astype(BF16)

    half_rows = PROJ_ROWS // ROW_SPLIT

    def sub_dot(c, rh):
        rows = slice(rh * half_rows, (rh + 1) * half_rows)
        return lax.dot_general(x_ref[rows, :], w_s[c * SUB_COLS:(c + 1) * SUB_COLS, :], NT_DIMS,
                               preferred_element_type=F32)

    top = stages[0].shape[1] - PROJ_ROWS
    slot0 = lax.shift_right_logical(i, 30)

    def emit(val, o_ref, col, c, h, dil, rh):
        rows = slice(rh * half_rows, (rh + 1) * half_rows)
        if dil == 1:
            o_ref[rows, col:col + LANES] = val.astype(o_ref.dtype)
            return
        slot = c * (SUB_COLS // LANES) + h
        slab = deint_ref.at[0, slot]
        slab[rows, :] = val
        if rh < ROW_SPLIT - 1:
            return
        groups = 1
        if dil != DEINT_STRIDE:
            assert dil == DEINT_STRIDE ** 2
            by_low = deint_ref.at[1, slot]
            cnt = PROJ_ROWS // DEINT_STRIDE
            for low in range(DEINT_STRIDE):
                by_low[low * cnt:(low + 1) * cnt, :] = slab[pl.ds(low, cnt, stride=DEINT_STRIDE), :]
            slab, groups = by_low, DEINT_STRIDE
        per_group = PROJ_ROWS // groups
        cnt = PROJ_ROWS // dil
        for low in range(groups):
            for high in range(DEINT_STRIDE):
                r = high * groups + low
                o_ref[r * cnt:(r + 1) * cnt, col:col + LANES] = (
                    slab[pl.ds(low * per_group + high, cnt, stride=DEINT_STRIDE), :].astype(o_ref.dtype))

    def epilogue(kind, c, rh, o_ref):
        kind, dil = kind if isinstance(kind, tuple) else (kind, 1)
        lc = c % SUBS_PER_HALF
        cols = slice(lc * SUB_COLS, (lc + 1) * SUB_COLS)
        rows = slice(rh * half_rows, (rh + 1) * half_rows)
        if kind == "plain" and dil > 1:
            acc = sub_dot(c, rh)
            for h in range(SUB_COLS // LANES):
                emit(acc[:, h * LANES:(h + 1) * LANES], o_ref, lc * SUB_COLS + h * LANES, c, h, dil, rh)
        elif kind == "plain":
            o_ref[rows, cols] = sub_dot(c, rh).astype(o_ref.dtype)
        elif kind == "silu":
            o_ref[rows, cols] = _silu(sub_dot(c, rh)).astype(o_ref.dtype)
        elif kind == "conv":
            st = stages[c % 2].at[c // 2]
            if rh == 0:
                first = (i % (SEQ // PROJ_ROWS)) == 0
                last = st[PROJ_ROWS:PROJ_ROWS + top, :]
                st[0:top, :] = jnp.where(first, 0.0, last)
            lo = top + rh * half_rows
            st[lo:lo + half_rows, :] = sub_dot(c, rh)
            rd = stages[c % 2].at[slot0 + c // 2]
            wcols = slice(c * SUB_COLS, (c + 1) * SUB_COLS)
            w = cw_ref[:, wcols]
            y = rd[lo:lo + half_rows, :] * w[SSD_CONV - 1:SSD_CONV, :] + cb_ref[:, wcols]
            for s in range(1, SSD_CONV):
                y = y + rd[lo - s:lo - s + half_rows, :] * w[SSD_CONV - 1 - s:SSD_CONV - s, :]
            o_ref[rows, cols] = _silu(y).astype(o_ref.dtype)
        else:
            acc = sub_dot(c, rh)
            cos = cos_ref[rows, :]
            sin = sin_ref[rows, :]
            scale = Q_SCALE * LOG2_E if kind == "rope_q" else 1.0
            lane = lax.broadcasted_iota(jnp.int32, (half_rows, LANES), 1)
            for h in range(SUB_COLS // HEAD_DIM):
                t = acc[:, h * HEAD_DIM:(h + 1) * HEAD_DIM]
                partner = jnp.where(lane < ROPE_HALF,
                                    pltpu.roll(t, LANES - ROPE_HALF, 1),
                                    pltpu.roll(t, ROPE_HALF, 1))
                emit((t * cos + partner * sin) * scale, o_ref, lc * SUB_COLS + h * HEAD_DIM, c, h, dil, rh)

    for lo_step, hi_step, kind_a, kind_b in plan:
        @pl.when((n >= lo_step) & (n < hi_step))
        def _(kind_a=kind_a, kind_b=kind_b):
            for rh in range(ROW_SPLIT):
                for c in range(SUBS_PER_HALF):
                    epilogue(kind_a, c, rh, oa_ref)
                for c in range(SUBS_PER_HALF, 2 * SUBS_PER_HALF):
                    epilogue(kind_b, c, rh, ob_ref)


_W_Q, _W_K, _W_V = 0, 6, 12
_W_ZATTN, _W_ZSSD, _W_XBC = 18, 20, 28
_MAIN = dict(
    wa=[_W_Q, _W_Q + 1, _W_K, _W_K + 1] + [_W_ZSSD + 2 * k for k in range(4)]
    + [_W_XBC + 2 * k for k in range(6)],
    wb=[_W_V, _W_V + 1, _W_ZATTN, _W_ZATTN + 1] + [_W_ZSSD + 2 * k + 1 for k in range(4)]
    + [_W_XBC + 2 * k + 1 for k in range(6)],
    plan=((0, 2, "rope_q", "plain"), (2, 4, "rope_k", "silu"), (4, 8, "silu", "silu"),
          (8, 14, "conv", "conv")),
    conv_first=8, split_out=False, dtype=BF16)
_DILATED = dict(
    wa=[_W_Q + 2, _W_Q + 3, _W_Q + 4, _W_Q + 5, _W_K + 2, _W_K + 3],
    wb=[_W_V + 2, _W_V + 3, _W_V + 4, _W_V + 5, _W_K + 4, _W_K + 5],
    plan=((0, 2, ("rope_q", 4), ("plain", 4)), (2, 4, ("rope_q", 16), ("plain", 16)),
          (4, 6, ("rope_k", 4), ("rope_k", 16))),
    conv_first=0, split_out=True, dtype=BF16)


def _inproj(xn, w_t, cos, sin, conv_w, conv_b, cfg, name):
    steps = len(cfg["wa"])
    t_spec = pl.BlockSpec((PROJ_ROWS, LANES), lambda n, i: (i, 0))
    conv_col = lambda n, i: (0, jnp.clip(n - cfg["conv_first"], 0, SSD_CONV_CH // COL_TILE - 1))
    w_spec = lambda rows: pl.BlockSpec((None, HALF_TILE, D_MODEL), lambda n, i: (0, _lookup(n, rows), 0))
    if cfg["split_out"]:
        out_specs = [pl.BlockSpec((PROJ_ROWS, HALF_TILE), lambda n, i: (i, n))] * 2
        out_shape = [jax.ShapeDtypeStruct((TOKENS, steps * HALF_TILE), cfg["dtype"])] * 2
    else:
        out_specs = pl.BlockSpec((PROJ_ROWS, COL_TILE), lambda n, i: (i, n))
        out_shape = jax.ShapeDtypeStruct((TOKENS, steps * COL_TILE), cfg["dtype"])
    stage = pltpu.VMEM((2, 8 + PROJ_ROWS, SUB_COLS), F32)
    return pl.pallas_call(
        functools.partial(_inproj_kernel, plan=cfg["plan"], split_out=cfg["split_out"]),
        grid=(steps, TOKENS // PROJ_ROWS),
        in_specs=[pl.BlockSpec((PROJ_ROWS, D_MODEL), lambda n, i: (i, 0)),
                  w_spec(cfg["wa"]), w_spec(cfg["wb"]), t_spec, t_spec,
                  pl.BlockSpec((SSD_CONV, COL_TILE), conv_col),
                  pl.BlockSpec((1, COL_TILE), conv_col)],
        out_specs=out_specs,
        out_shape=out_shape,
        scratch_shapes=[pltpu.VMEM((COL_TILE, D_MODEL), BF16),
                        pltpu.VMEM((2, COL_TILE // LANES, PROJ_ROWS, LANES), F32)]
        + [stage] * 2,
        compiler_params=_params("arbitrary", "arbitrary"),
        name=name,
    )(xn, w_t, w_t, cos, sin, conv_w, conv_b)


def _gate_kernel(x_ref, wa_ref, wb_ref, b_ref, dtb_col_ref, o_ref, dt_ref, dtt_ref, w_s):
    @pl.when(pl.program_id(1) == 0)
    def _():
        w_s[0:COL_TILE - SSD_HEADS, :] = wa_ref[SSD_HEADS:COL_TILE, :].astype(BF16)
        w_s[COL_TILE - SSD_HEADS:COL_TILE, :] = wb_ref[...].astype(BF16)

    @pl.when(pl.program_id(0) == 0)
    def _():
        w = wa_ref[0:SSD_HEADS, :].astype(BF16)
        dtt = jax.nn.softplus(
            lax.dot_general(w, x_ref[...], NT_DIMS, preferred_element_type=F32) + dtb_col_ref[...])
        dtt_ref[...] = dtt
        dt_ref[...] = dtt.T

    half_rows = PROJ_ROWS // ROW_SPLIT
    for rh in range(ROW_SPLIT):
        rows = slice(rh * half_rows, (rh + 1) * half_rows)
        for c in range(COL_TILE // SUB_COLS):
            cols = slice(c * SUB_COLS, (c + 1) * SUB_COLS)
            acc = lax.dot_general(x_ref[rows, :], w_s[cols, :], NT_DIMS, preferred_element_type=F32)
            logits = acc + b_ref[:, cols]
            o_ref[rows, cols] = (0.5 + 0.5 * jnp.tanh(0.5 * logits)).astype(o_ref.dtype)


def _gate_proj(xn, w_t, gate_bias, dt_bias):
    n_out = 2 * D_MODEL
    first = COL_DT // COL_TILE
    per = COL_TILE // SSD_HEADS
    n_rows = TOKENS // PROJ_ROWS
    dt_row = lambda n, i: jnp.where(n == 0, i, n_rows - 1)
    return pl.pallas_call(
        _gate_kernel,
        grid=(n_out // COL_TILE, n_rows),
        in_specs=[pl.BlockSpec((PROJ_ROWS, D_MODEL), lambda n, i: (i, 0)),
                  pl.BlockSpec((None, COL_TILE, D_MODEL), lambda n, i: (0, first + n, 0)),
                  pl.BlockSpec((None, SSD_HEADS, D_MODEL), lambda n, i: (0, (first + n + 1) * per, 0)),
                  pl.BlockSpec((1, COL_TILE), lambda n, i: (0, n)),
                  pl.BlockSpec((SSD_HEADS, 1), lambda n, i: (0, 0))],
        out_specs=[pl.BlockSpec((PROJ_ROWS, COL_TILE), lambda n, i: (i, n)),
                   pl.BlockSpec((PROJ_ROWS, SSD_HEADS), lambda n, i: (dt_row(n, i), 0)),
                   pl.BlockSpec((SSD_HEADS, PROJ_ROWS), lambda n, i: (0, dt_row(n, i)))],
        out_shape=[jax.ShapeDtypeStruct((TOKENS, n_out), BF16),
                   jax.ShapeDtypeStruct((TOKENS, SSD_HEADS), F32),
                   jax.ShapeDtypeStruct((SSD_HEADS, TOKENS), F32)],
        scratch_shapes=[pltpu.VMEM((COL_TILE, D_MODEL), BF16)],
        compiler_params=_params("arbitrary", "arbitrary"),
        name="gate_proj",
    )(xn, w_t, w_t, gate_bias, dt_bias.reshape(SSD_HEADS, 1))


def _attn_kernel(q0, k0, v0, q1, k1, v1, q2, k2, v2, z_ref, o_ref, m_s, a_s, d_s, *perm):
    blk = ATTN_BLOCK
    qi = lax.broadcasted_iota(jnp.int32, (blk, blk), 0)
    ki = lax.broadcasted_iota(jnp.int32, (blk, blk), 1)
    mask_cur = ki <= qi
    mask_prev = ki >= qi
    nb = SEQ // blk
    qk_dims = (((2,), (2,)), ((0,), (0,)))
    pv_dims = (((2,), (1,)), ((0,), (0,)))

    for src, dst, dil in zip((q1, k1, v1, q2, k2, v2), perm, (DILATIONS[0],) * 3 + (DILATIONS[1],) * 3):
        sub = SEQ // dil
        cnt = PROJ_ROWS // dil
        for r in range(dil):
            for blk_i in range(SEQ // PROJ_ROWS):
                dst[r * sub + blk_i * cnt:r * sub + (blk_i + 1) * cnt, :] = (
                    src[blk_i * PROJ_ROWS + r * cnt:blk_i * PROJ_ROWS + (r + 1) * cnt, :])

    def group(q_ref, k_ref, v_ref, blocks_per_seq):
        q = q_ref[...].reshape(nb, blk, HEAD_DIM)
        k = k_ref[...].reshape(nb, blk, HEAD_DIM)
        v = jnp.concatenate([v_ref[...], jnp.ones((SEQ, LANES), BF16)], axis=1).reshape(nb, blk, -1)
        s_cur = lax.dot_general(q, k, qk_dims, preferred_element_type=F32)
        s_cur = jnp.where(mask_cur[None], s_cur, -jnp.inf)
        m = jnp.max(s_cur, axis=-1, keepdims=True)
        if blocks_per_seq > 1:
            k_prev = jnp.concatenate([k[:1], k[:-1]], axis=0)
            v_prev = jnp.concatenate([v[:1], v[:-1]], axis=0)
            b_idx = lax.broadcasted_iota(jnp.int32, (nb, 1, 1), 0)
            no_prev = jnp.where(b_idx % blocks_per_seq == 0, -jnp.inf, 0.0)
            s_prev = lax.dot_general(q, k_prev, qk_dims, preferred_element_type=F32)
            s_prev = jnp.where(mask_prev[None], s_prev, -jnp.inf) + no_prev
            m = jnp.maximum(m, jnp.max(s_prev, axis=-1, keepdims=True))
        p = jnp.exp2(s_cur - m)
        acc = lax.dot_general(p.astype(BF16), v, pv_dims, preferred_element_type=F32)
        if blocks_per_seq > 1:
            p = jnp.exp2(s_prev - m)
            acc = acc + lax.dot_general(p.astype(BF16), v_prev, pv_dims, preferred_element_type=F32)
        acc = acc.reshape(SEQ, HEAD_DIM + LANES)
        m = jnp.broadcast_to(m, (nb, blk, LANES)).reshape(SEQ, LANES)
        return m, acc[:, HEAD_DIM:], acc[:, :HEAD_DIM]

    for buf, val in zip((m_s, d_s, a_s), group(q0, k0, v0, nb)):
        buf[0] = val
    for g, dil in enumerate(DILATIONS):
        sub = SEQ // dil
        for buf, val in zip((m_s, d_s, a_s), group(*perm[3 * g:3 * g + 3], sub // blk)):
            for r in range(dil):
                buf[g + 1, pl.ds(r, sub, stride=dil), :] = val[r * sub:(r + 1) * sub, :]

    n_groups = len(DILATIONS) + 1
    m_all = [m_s[g] for g in range(n_groups)]
    m_max = functools.reduce(jnp.maximum, m_all)
    w = [jnp.exp2(m - m_max) for m in m_all]
    num = sum(w[g] * a_s[g] for g in range(n_groups))
    den = sum(w[g] * d_s[g] for g in range(n_groups))
    o_ref[...] = (num / den * z_ref[...].astype(F32)).astype(o_ref.dtype)


def _attention(p_main, dil_a, dil_b):
    hb = HEADS_PER_GROUP

    def spec(sec):
        return pl.BlockSpec((SEQ, HEAD_DIM), lambda b, h: (b, sec * hb + h))

    def main_spec(first):
        per = HALF_TILE // HEAD_DIM
        return pl.BlockSpec((SEQ, HEAD_DIM), lambda b, h: (b, first * per + h + per * (h // per)))

    return pl.pallas_call(
        _attn_kernel,
        grid=(BATCH, hb),
        in_specs=[main_spec(0), main_spec(4), main_spec(1), spec(0), spec(2), spec(0), spec(1), spec(2),
                  spec(1), main_spec(5)],
        out_specs=pl.BlockSpec((SEQ, HEAD_DIM), lambda b, h: (b, h)),
        out_shape=jax.ShapeDtypeStruct((TOKENS, GROUP_WIDTH), BF16),
        scratch_shapes=([pltpu.VMEM((len(DILATIONS) + 1, SEQ, LANES), F32)] * 3
                        + [pltpu.VMEM((SEQ, HEAD_DIM), BF16)] * 6),
        compiler_params=_params("arbitrary", "arbitrary"),
        name="dilated_attn",
    )(p_main, p_main, p_main, dil_a, dil_a, dil_b, dil_a, dil_b, dil_b, p_main)


def _ssd_kernel(xs_ref, b_ref, c_ref, z_ref, dt_ref, dtt_ref, arow_ref, acol_ref, dskip_ref, nw_ref,
                o_ref, state, y_s, cs3, cst_s, wrow_s, cdec_s):
    L = SSD_CHUNK
    G = SSD_GROUPS
    J = SSD_HEADS_PER_GROUP
    GW = SSD_GROUP_WIDTH
    N = SSD_STATE

    @pl.when(pl.program_id(1) == 0)
    def _():
        state[...] = jnp.zeros_like(state)

    li = lax.broadcasted_iota(jnp.int32, (L, L), 0)
    si = lax.broadcasted_iota(jnp.int32, (L, L), 1)
    tri = li >= si
    a_row = -jnp.exp(arow_ref[...])
    a_col = -jnp.exp(acol_ref[...])
    dtt = dtt_ref[...]
    cs = LOG2_E * jnp.dot(tri.astype(F32), dt_ref[...] * a_row, precision=HIGHEST,
                          preferred_element_type=F32)
    cst = LOG2_E * jnp.dot(dtt * a_col, (si >= li).astype(F32), precision=HIGHEST,
                           preferred_element_type=F32)
    cs_last = cst[:, L - 1:L]
    cst_s[...] = cst - jnp.log2(dtt)
    wrow_s[...] = dtt * jnp.exp2(cs_last - cst)
    cdec_s[...] = jnp.broadcast_to(jnp.exp2(cs_last), (SSD_HEADS, LANES))
    for g in range(G):
        cs3[g] = cs[:, g * J:(g + 1) * J]

    lane = lax.broadcasted_iota(jnp.int32, (L, LANES), 1)
    lo_half = lane < SSD_HEAD_DIM

    def split(v):
        zero = jnp.zeros_like(v)
        return jnp.where(lo_half, v, zero), jnp.where(lo_half, zero, v)

    def group(g):
        cols = slice(g * GW, (g + 1) * GW)
        ncols = slice(g * N, (g + 1) * N)
        heads = slice(g * J, (g + 1) * J)
        x_g = xs_ref[:, cols]
        b_g = b_ref[:, ncols]
        c_g = c_ref[:, ncols]
        cb = lax.dot_general(c_g, b_g, NT_DIMS, preferred_element_type=F32)
        bt = b_g.astype(F32).T
        cf = c_g.astype(F32)
        cs_g = cs3[g]
        cst_g = cst_s[heads, :]
        w_g = wrow_s[heads, :]
        cd_g = cdec_s[heads, :]
        st_prev = state[g]
        y_parts = []
        s_parts = []
        for p in range(J // 2):
            lhs_y = []
            lhs_s = []
            for j in (2 * p, 2 * p + 1):
                col = jnp.broadcast_to(cs_g[:, j:j + 1], (L, LANES))
                decay = jnp.exp2(jnp.where(tri, col - cst_g[j:j + 1, :], -jnp.inf))
                lhs_y.append((cb * decay).astype(BF16))
                lhs_y.append((cf * jnp.exp2(col)).astype(BF16))
                lhs_s.append((bt * w_g[j:j + 1, :]).astype(BF16))
            lanes_p = slice(p * LANES, (p + 1) * LANES)
            x_lo, x_hi = split(x_g[:, lanes_p])
            s_p = st_prev[:, lanes_p]
            s_lo, s_hi = split(s_p.astype(BF16))
            y_parts.append(jnp.dot(jnp.concatenate(lhs_y, axis=1),
                                   jnp.concatenate([x_lo, s_lo, x_hi, s_hi], axis=0),
                                   preferred_element_type=F32))
            cd = jnp.where(lo_half[0:1, :], cd_g[2 * p:2 * p + 1, :], cd_g[2 * p + 1:2 * p + 2, :])
            s_parts.append(s_p * cd + jnp.dot(jnp.concatenate(lhs_s, axis=1),
                                              jnp.concatenate([x_lo, x_hi], axis=0),
                                              preferred_element_type=F32))
        state[g] = jnp.concatenate(s_parts, axis=1)
        y = jnp.concatenate(y_parts, axis=1) + x_g.astype(F32) * dskip_ref[:, cols]
        y_s[:, cols] = y * z_ref[:, cols].astype(F32)

    for g in range(G):
        group(g)

    y = y_s[...]
    ms = jnp.mean(y * y, axis=-1, keepdims=True)
    o_ref[...] = (y * lax.rsqrt(ms + NORM_EPS) * nw_ref[...]).astype(o_ref.dtype)


def _ssd(p_main, dt, dt_t, a_log, d_skip, norm_w):
    L = SSD_CHUNK
    row = lambda b, c: b * N_CHUNKS + c
    const = lambda b, c: (0, 0)
    return pl.pallas_call(
        _ssd_kernel,
        grid=(BATCH, N_CHUNKS),
        in_specs=[
            pl.BlockSpec((L, SSD_INNER), lambda b, c: (row(b, c), 2)),
            pl.BlockSpec((L, SSD_BC_WIDTH), lambda b, c: (row(b, c), 12)),
            pl.BlockSpec((L, SSD_BC_WIDTH), lambda b, c: (row(b, c), 13)),
            pl.BlockSpec((L, SSD_INNER), lambda b, c: (row(b, c), 1)),
            pl.BlockSpec((L, SSD_HEADS), lambda b, c: (row(b, c), 0)),
            pl.BlockSpec((SSD_HEADS, L), lambda b, c: (0, row(b, c))),
            pl.BlockSpec((1, SSD_HEADS), const),
            pl.BlockSpec((SSD_HEADS, 1), const),
            pl.BlockSpec((1, SSD_INNER), const),
            pl.BlockSpec((1, SSD_INNER), const),
        ],
        out_specs=pl.BlockSpec((L, SSD_INNER), lambda b, c: (row(b, c), 0)),
        out_shape=jax.ShapeDtypeStruct((TOKENS, SSD_INNER), BF16),
        scratch_shapes=[
            pltpu.VMEM((SSD_GROUPS, SSD_STATE, SSD_GROUP_WIDTH), F32),
            pltpu.VMEM((L, SSD_INNER), F32),
            pltpu.VMEM((SSD_GROUPS, L, SSD_HEADS_PER_GROUP), F32),
            pltpu.VMEM((SSD_HEADS, L), F32),
            pltpu.VMEM((SSD_HEADS, L), F32),
            pltpu.VMEM((SSD_HEADS, LANES), F32),
        ],
        compiler_params=_params("arbitrary", "arbitrary"),
        name="ssd",
    )(p_main, p_main, p_main, p_main, dt, dt_t,
      a_log.reshape(1, SSD_HEADS), a_log.reshape(SSD_HEADS, 1),
      jnp.repeat(d_skip, SSD_HEAD_DIM).reshape(1, SSD_INNER), norm_w.reshape(1, SSD_INNER))


def _branch_kernel(a_ref, s_ref, wa_ref, ws_ref, g0_ref, g1_ref, o_ref, wab, wsb):
    @pl.when(pl.program_id(1) == 0)
    def _():
        wab[...] = wa_ref[...].astype(BF16)
        wsb[...] = ws_ref[...].astype(BF16)

    for c in range(o_ref.shape[1] // SUB_COLS):
        cols = slice(c * SUB_COLS, (c + 1) * SUB_COLS)
        ya = jnp.dot(a_ref[...], wab[:, cols], preferred_element_type=F32)
        ys = jnp.dot(s_ref[...], wsb[:, cols], preferred_element_type=F32)
        merged = g0_ref[:, cols].astype(F32) * ya + g1_ref[:, cols].astype(F32) * ys
        o_ref[:, cols] = merged.astype(o_ref.dtype)


def _branches(attn, ssd, w_attn, w_ssd, gates):
    tn = 512
    nj = D_MODEL // tn
    return pl.pallas_call(
        _branch_kernel,
        grid=(nj, TOKENS // PROJ_ROWS),
        in_specs=[pl.BlockSpec((PROJ_ROWS, GROUP_WIDTH), lambda n, i: (i, 0)),
                  pl.BlockSpec((PROJ_ROWS, SSD_INNER), lambda n, i: (i, 0)),
                  pl.BlockSpec((None, GROUP_WIDTH, tn), lambda n, i: (0, 0, n)),
                  pl.BlockSpec((None, SSD_INNER, tn), lambda n, i: (0, 0, n)),
                  pl.BlockSpec((PROJ_ROWS, tn), lambda n, i: (i, n)),
                  pl.BlockSpec((PROJ_ROWS, tn), lambda n, i: (i, nj + n))],
        out_specs=pl.BlockSpec((PROJ_ROWS, tn), lambda n, i: (i, n)),
        out_shape=jax.ShapeDtypeStruct((TOKENS, D_MODEL), BF16),
        scratch_shapes=[pltpu.VMEM((GROUP_WIDTH, tn), BF16), pltpu.VMEM((SSD_INNER, tn), BF16)],
        compiler_params=_params("arbitrary", "arbitrary"),
        name="branches",
    )(attn, ssd, w_attn, w_ssd, gates, gates)


def _out_kernel(m_ref, w_ref, x_ref, fw_ref, o_ref):
    sub = 2 * SUB_COLS
    ssq = jnp.zeros((o_ref.shape[0], 1), F32)
    for c in range(D_MODEL // sub):
        cols = slice(c * sub, (c + 1) * sub)
        h = x_ref[:, cols] + jnp.dot(m_ref[...], w_ref[:, cols], preferred_element_type=F32)
        ssq = ssq + jnp.sum(h * h, axis=-1, keepdims=True)
        o_ref[:, cols] = h
    o_ref[...] = o_ref[...] * lax.rsqrt(ssq / D_MODEL + NORM_EPS) * fw_ref[...]


def _out_proj(merged, w_out_bf, x2, final_w):
    tm = ROW_TILE
    return pl.pallas_call(
        _out_kernel,
        grid=(TOKENS // tm,),
        in_specs=[pl.BlockSpec((tm, D_MODEL), lambda i: (i, 0)),
                  pl.BlockSpec((D_MODEL, D_MODEL), lambda i: (0, 0)),
                  pl.BlockSpec((tm, D_MODEL), lambda i: (i, 0)),
                  pl.BlockSpec((1, D_MODEL), lambda i: (0, 0))],
        out_specs=pl.BlockSpec((tm, D_MODEL), lambda i: (i, 0)),
        out_shape=jax.ShapeDtypeStruct((TOKENS, D_MODEL), F32),
        compiler_params=_params("arbitrary"),
        name="out_proj",
    )(merged, w_out_bf, x2, final_w.reshape(1, D_MODEL))


def kernel(x, positions, norm_w, w_in, conv_w, conv_b, dt_bias, a_log, d_skip, ssd_norm_w,
           w_attn_br, w_ssd_br, gate_b, w_out, final_norm_w):
    assert x.shape == (BATCH, SEQ, D_MODEL) and w_in.shape[0] == 1
    x2 = x.reshape(TOKENS, D_MODEL)
    xn, cos, sin = _prep(x2, norm_w[0], positions)
    w_t = jnp.swapaxes(w_in, 1, 2)
    cw, cb = conv_w[0], conv_b.reshape(1, SSD_CONV_CH)
    p_main = _inproj(xn, w_t, cos, sin, cw, cb, _MAIN, "inproj_main")
    dil_a, dil_b = _inproj(xn, w_t, cos, sin, cw, cb, _DILATED, "inproj_dilated")
    gates, dt, dt_t = _gate_proj(xn, w_t, gate_b[0].reshape(1, 2 * D_MODEL), dt_bias[0])
    attn = _attention(p_main, dil_a, dil_b)
    ssd = _ssd(p_main, dt, dt_t, a_log[0], d_skip[0], ssd_norm_w[0])
    merged = _branches(attn, ssd, w_attn_br, w_ssd_br, gates)
    out = _out_proj(merged, w_out[0].astype(BF16), x2, final_norm_w)
    return out.reshape(BATCH, SEQ, D_MODEL)
```

```python
import functools

import jax
import jax.numpy as jnp
from jax import lax
from jax.experimental import pallas as pl
from jax.experimental.pallas import tpu as pltpu

F32 = jnp.float32
BF16 = jnp.bfloat16
HIGHEST = lax.Precision.HIGHEST

D_MODEL = 2048
BATCH = 4
SEQ = 2048
TOKENS = BATCH * SEQ
NORM_EPS = 1e-5

HEAD_DIM = 128
HEADS_PER_GROUP = 8
GROUP_WIDTH = HEADS_PER_GROUP * HEAD_DIM
ATTN_BLOCK = 128
DILATIONS = (4, 16)
ROPE_DIMS = HEAD_DIM // 4
ROPE_HALF = ROPE_DIMS // 2
ROPE_THETA = 500000.0
Q_SCALE = HEAD_DIM ** -0.5
LOG2_E = 1.4426950408889634

SSD_INNER = 2 * D_MODEL
SSD_HEAD_DIM = 64
SSD_HEADS = SSD_INNER // SSD_HEAD_DIM
SSD_GROUPS = 8
SSD_HEADS_PER_GROUP = SSD_HEADS // SSD_GROUPS
SSD_STATE = 128
SSD_CONV = 4
SSD_CHUNK = 128
SSD_GROUP_WIDTH = SSD_HEADS_PER_GROUP * SSD_HEAD_DIM
SSD_BC_WIDTH = SSD_GROUPS * SSD_STATE
SSD_CONV_CH = SSD_INNER + 2 * SSD_BC_WIDTH
N_CHUNKS = SEQ // SSD_CHUNK

COL_DT = 3 * 3 * GROUP_WIDTH + GROUP_WIDTH + SSD_INNER + SSD_CONV_CH
COL_GATES = COL_DT + SSD_HEADS

LANES = 128
ROW_TILE = 512
PROJ_ROWS = 1024
COL_TILE = 1024
HALF_TILE = COL_TILE // 2
SUB_COLS = 256
SUBS_PER_HALF = HALF_TILE // SUB_COLS
ROW_SPLIT = 2
DEINT_STRIDE = 4
VMEM_LIMIT = 56 * 1024 * 1024


NT_DIMS = (((1,), (1,)), ((), ()))


def _params(*sem):
    return pltpu.CompilerParams(dimension_semantics=sem, vmem_limit_bytes=VMEM_LIMIT)


def _prep_kernel(x_ref, w_ref, pos_ref, invf_ref, xn_ref, cos_ref, sin_ref):
    x = x_ref[...]
    ms = jnp.mean(x * x, axis=-1, keepdims=True)
    xn_ref[...] = (x * lax.rsqrt(ms + NORM_EPS) * w_ref[...]).astype(xn_ref.dtype)
    ang = pos_ref[...].astype(F32) * invf_ref[...]
    lane = lax.broadcasted_iota(jnp.int32, ang.shape, 1)
    sign = jnp.where(lane < ROPE_HALF, -1.0, jnp.where(lane < ROPE_DIMS, 1.0, 0.0))
    cos_ref[...] = jnp.cos(ang)
    sin_ref[...] = jnp.sin(ang) * sign


def _prep(x2, w, positions):
    inv_freq = ROPE_THETA ** (-jnp.arange(ROPE_HALF, dtype=F32) / ROPE_HALF)
    invf = jnp.zeros((1, LANES), F32).at[0, :ROPE_DIMS].set(jnp.tile(inv_freq, 2))
    row_spec = lambda width: pl.BlockSpec((ROW_TILE, width), lambda i: (i, 0))
    const_spec = lambda width: pl.BlockSpec((1, width), lambda i: (0, 0))
    return pl.pallas_call(
        _prep_kernel,
        grid=(TOKENS // ROW_TILE,),
        in_specs=[row_spec(D_MODEL), const_spec(D_MODEL), row_spec(1), const_spec(LANES)],
        out_specs=[row_spec(D_MODEL), row_spec(LANES), row_spec(LANES)],
        out_shape=[jax.ShapeDtypeStruct((TOKENS, D_MODEL), BF16),
                   jax.ShapeDtypeStruct((TOKENS, LANES), F32),
                   jax.ShapeDtypeStruct((TOKENS, LANES), F32)],
        compiler_params=_params("arbitrary"),
        name="prep",
    )(x2, w.reshape(1, D_MODEL), positions.reshape(TOKENS, 1), invf)


def _silu(v):
    return 0.5 * v * (1.0 + jnp.tanh(0.5 * v))


def _lookup(n, values):
    out = values[-1]
    for k in range(len(values) - 2, -1, -1):
        out = jnp.where(n == k, values[k], out)
    return out


def _inproj_kernel(x_ref, wa_ref, wb_ref, cos_ref, sin_ref, cw_ref, cb_ref, *refs, plan, split_out):
    if split_out:
        oa_ref, ob_ref, w_s, deint_ref, *stages = refs
    else:
        o_ref, w_s, deint_ref, *stages = refs
        oa_ref = o_ref.at[:, 0:HALF_TILE]
        ob_ref = o_ref.at[:, HALF_TILE:COL_TILE]
    n = pl.program_id(0)
    i = pl.program_id(1)

    @pl.when(i == 0)
    def _():
        w_s[0:HALF_TILE, :] = wa_ref[...].astype(BF16)
        w_s[HALF_TILE:COL_TILE, :] = wb_ref[...].astype(BF16)

    half_rows = PROJ_ROWS // ROW_SPLIT

    def sub_dot(c, rh):
        rows = slice(rh * half_rows, (rh + 1) * half_rows)
        return lax.dot_general(x_ref[rows, :], w_s[c * SUB_COLS:(c + 1) * SUB_COLS, :], NT_DIMS,
                               preferred_element_type=F32)

    top = stages[0].shape[1] - PROJ_ROWS
    slot0 = lax.shift_right_logical(i, 30)

    def emit(val, o_ref, col, c, h, dil, rh):
        rows = slice(rh * half_rows, (rh + 1) * half_rows)
        if dil == 1:
            o_ref[rows, col:col + LANES] = val.astype(o_ref.dtype)
            return
        slot = c * (SUB_COLS // LANES) + h
        slab = deint_ref.at[0, slot]
        slab[rows, :] = val
        if rh < ROW_SPLIT - 1:
            return
        groups = 1
        if dil != DEINT_STRIDE:
            assert dil == DEINT_STRIDE ** 2
            by_low = deint_ref.at[1, slot]
            cnt = PROJ_ROWS // DEINT_STRIDE
            for low in range(DEINT_STRIDE):
                by_low[low * cnt:(low + 1) * cnt, :] = slab[pl.ds(low, cnt, stride=DEINT_STRIDE), :]
            slab, groups = by_low, DEINT_STRIDE
        per_group = PROJ_ROWS // groups
        cnt = PROJ_ROWS // dil
        for low in range(groups):
            for high in range(DEINT_STRIDE):
                r = high * groups + low
                o_ref[r * cnt:(r + 1) * cnt, col:col + LANES] = (
                    slab[pl.ds(low * per_group + high, cnt, stride=DEINT_STRIDE), :].astype(o_ref.dtype))

    def epilogue(kind, c, rh, o_ref):
        kind, dil = kind if isinstance(kind, tuple) else (kind, 1)
        lc = c % SUBS_PER_HALF
        cols = slice(lc * SUB_COLS, (lc + 1) * SUB_COLS)
        rows = slice(rh * half_rows, (rh + 1) * half_rows)
        if kind == "plain" and dil > 1:
            acc = sub_dot(c, rh)
            for h in range(SUB_COLS // LANES):
                emit(acc[:, h * LANES:(h + 1) * LANES], o_ref, lc * SUB_COLS + h * LANES, c, h, dil, rh)
        elif kind == "plain":
            o_ref[rows, cols] = sub_dot(c, rh).astype(o_ref.dtype)
        elif kind == "silu":
            o_ref[rows, cols] = _silu(sub_dot(c, rh)).astype(o_ref.dtype)
        elif kind == "conv":
            st = stages[c % 2].at[c // 2]
            if rh == 0:
                first = (i % (SEQ // PROJ_ROWS)) == 0
                last = st[PROJ_ROWS:PROJ_ROWS + top, :]
                st[0:top, :] = jnp.where(first, 0.0, last)
            lo = top + rh * half_rows
            st[lo:lo + half_rows, :] = sub_dot(c, rh)
            rd = stages[c % 2].at[slot0 + c // 2]
            wcols = slice(c * SUB_COLS, (c + 1) * SUB_COLS)
            w = cw_ref[:, wcols]
            y = rd[lo:lo + half_rows, :] * w[SSD_CONV - 1:SSD_CONV, :] + cb_ref[:, wcols]
            for s in range(1, SSD_CONV):
                y = y + rd[lo - s:lo - s + half_rows, :] * w[SSD_CONV - 1 - s:SSD_CONV - s, :]
            o_ref[rows, cols] = _silu(y).astype(o_ref.dtype)
        else:
            acc = sub_dot(c, rh)
            cos = cos_ref[rows, :]
            sin = sin_ref[rows, :]
            scale = Q_SCALE * LOG2_E if kind == "rope_q" else 1.0
            lane = lax.broadcasted_iota(jnp.int32, (half_rows, LANES), 1)
            for h in range(SUB_COLS // HEAD_DIM):
                t = acc[:, h * HEAD_DIM:(h + 1) * HEAD_DIM]
                partner = jnp.where(lane < ROPE_HALF,
                                    pltpu.roll(t, LANES - ROPE_HALF, 1),
                                    pltpu.roll(t, ROPE_HALF, 1))
                emit((t * cos + partner * sin) * scale, o_ref, lc * SUB_COLS + h * HEAD_DIM, c, h, dil, rh)

    for lo_step, hi_step, kind_a, kind_b in plan:
        @pl.when((n >= lo_step) & (n < hi_step))
        def _(kind_a=kind_a, kind_b=kind_b):
            for rh in range(ROW_SPLIT):
                for c in range(SUBS_PER_HALF):
                    epilogue(kind_a, c, rh, oa_ref)
                for c in range(SUBS_PER_HALF, 2 * SUBS_PER_HALF):
                    epilogue(kind_b, c, rh, ob_ref)


_W_Q, _W_K, _W_V = 0, 6, 12
_W_ZATTN, _W_ZSSD, _W_XBC = 18, 20, 28
_MAIN = dict(
    wa=[_W_Q, _W_Q + 1, _W_K, _W_K + 1] + [_W_ZSSD + 2 * k for k in range(4)]
    + [_W_XBC + 2 * k for k in range(6)],
    wb=[_W_V, _W_V + 1, _W_ZATTN, _W_ZATTN + 1] + [_W_ZSSD + 2 * k + 1 for k in range(4)]
    + [_W_XBC + 2 * k + 1 for k in range(6)],
    plan=((0, 2, "rope_q", "plain"), (2, 4, "rope_k", "silu"), (4, 8, "silu", "silu"),
          (8, 14, "conv", "conv")),
    conv_first=8, split_out=False, dtype=BF16)
_DILATED = dict(
    wa=[_W_Q + 2, _W_Q + 3, _W_Q + 4, _W_Q + 5, _W_K + 2, _W_K + 3],
    wb=[_W_V + 2, _W_V + 3, _W_V + 4, _W_V + 5, _W_K + 4, _W_K + 5],
    plan=((0, 2, ("rope_q", 4), ("plain", 4)), (2, 4, ("rope_q", 16), ("plain", 16)),
          (4, 6, ("rope_k", 4), ("rope_k", 16))),
    conv_first=0, split_out=True, dtype=BF16)


def _inproj(xn, w_t, cos, sin, conv_w, conv_b, cfg, name):
    steps = len(cfg["wa"])
    t_spec = pl.BlockSpec((PROJ_ROWS, LANES), lambda n, i: (i, 0))
    conv_col = lambda n, i: (0, jnp.clip(n - cfg["conv_first"], 0, SSD_CONV_CH // COL_TILE - 1))
    w_spec = lambda rows: pl.BlockSpec((None, HALF_TILE, D_MODEL), lambda n, i: (0, _lookup(n, rows), 0))
    if cfg["split_out"]:
        out_specs = [pl.BlockSpec((PROJ_ROWS, HALF_TILE), lambda n, i: (i, n))] * 2
        out_shape = [jax.ShapeDtypeStruct((TOKENS, steps * HALF_TILE), cfg["dtype"])] * 2
    else:
        out_specs = pl.BlockSpec((PROJ_ROWS, COL_TILE), lambda n, i: (i, n))
        out_shape = jax.ShapeDtypeStruct((TOKENS, steps * COL_TILE), cfg["dtype"])
    stage = pltpu.VMEM((2, 8 + PROJ_ROWS, SUB_COLS), F32)
    return pl.pallas_call(
        functools.partial(_inproj_kernel, plan=cfg["plan"], split_out=cfg["split_out"]),
        grid=(steps, TOKENS // PROJ_ROWS),
        in_specs=[pl.BlockSpec((PROJ_ROWS, D_MODEL), lambda n, i: (i, 0)),
                  w_spec(cfg["wa"]), w_spec(cfg["wb"]), t_spec, t_spec,
                  pl.BlockSpec((SSD_CONV, COL_TILE), conv_col),
                  pl.BlockSpec((1, COL_TILE), conv_col)],
        out_specs=out_specs,
        out_shape=out_shape,
        scratch_shapes=[pltpu.VMEM((COL_TILE, D_MODEL), BF16),
                        pltpu.VMEM((2, COL_TILE // LANES, PROJ_ROWS, LANES), F32)]
        + [stage] * 2,
        compiler_params=_params("arbitrary", "arbitrary"),
        name=name,
    )(xn, w_t, w_t, cos, sin, conv_w, conv_b)


def _gate_kernel(x_ref, wa_ref, wb_ref, b_ref, dtb_col_ref, o_ref, dt_ref, dtt_ref, w_s):
    @pl.when(pl.program_id(1) == 0)
    def _():
        w_s[0:COL_TILE - SSD_HEADS, :] = wa_ref[SSD_HEADS:COL_TILE, :].astype(BF16)
        w_s[COL_TILE - SSD_HEADS:COL_TILE, :] = wb_ref[...].astype(BF16)

    @pl.when(pl.program_id(0) == 0)
    def _():
        w = wa_ref[0:SSD_HEADS, :].astype(BF16)
        dtt = jax.nn.softplus(
            lax.dot_general(w, x_ref[...], NT_DIMS, preferred_element_type=F32) + dtb_col_ref[...])
        dtt_ref[...] = dtt
        dt_ref[...] = dtt.T

    half_rows = PROJ_ROWS // ROW_SPLIT
    for rh in range(ROW_SPLIT):
        rows = slice(rh * half_rows, (rh + 1) * half_rows)
        for c in range(COL_TILE // SUB_COLS):
            cols = slice(c * SUB_COLS, (c + 1) * SUB_COLS)
            acc = lax.dot_general(x_ref[rows, :], w_s[cols, :], NT_DIMS, preferred_element_type=F32)
            logits = acc + b_ref[:, cols]
            o_ref[rows, cols] = (0.5 + 0.5 * jnp.tanh(0.5 * logits)).astype(o_ref.dtype)


def _gate_proj(xn, w_t, gate_bias, dt_bias):
    n_out = 2 * D_MODEL
    first = COL_DT // COL_TILE
    per = COL_TILE // SSD_HEADS
    n_rows = TOKENS // PROJ_ROWS
    dt_row = lambda n, i: jnp.where(n == 0, i, n_rows - 1)
    return pl.pallas_call(
        _gate_kernel,
        grid=(n_out // COL_TILE, n_rows),
        in_specs=[pl.BlockSpec((PROJ_ROWS, D_MODEL), lambda n, i: (i, 0)),
                  pl.BlockSpec((None, COL_TILE, D_MODEL), lambda n, i: (0, first + n, 0)),
                  pl.BlockSpec((None, SSD_HEADS, D_MODEL), lambda n, i: (0, (first + n + 1) * per, 0)),
                  pl.BlockSpec((1, COL_TILE), lambda n, i: (0, n)),
                  pl.BlockSpec((SSD_HEADS, 1), lambda n, i: (0, 0))],
        out_specs=[pl.BlockSpec((PROJ_ROWS, COL_TILE), lambda n, i: (i, n)),
                   pl.BlockSpec((PROJ_ROWS, SSD_HEADS), lambda n, i: (dt_row(n, i), 0)),
                   pl.BlockSpec((SSD_HEADS, PROJ_ROWS), lambda n, i: (0, dt_row(n, i)))],
        out_shape=[jax.ShapeDtypeStruct((TOKENS, n_out), BF16),
                   jax.ShapeDtypeStruct((TOKENS, SSD_HEADS), F32),
                   jax.ShapeDtypeStruct((SSD_HEADS, TOKENS), F32)],
        scratch_shapes=[pltpu.VMEM((COL_TILE, D_MODEL), BF16)],
        compiler_params=_params("arbitrary", "arbitrary"),
        name="gate_proj",
    )(xn, w_t, w_t, gate_bias, dt_bias.reshape(SSD_HEADS, 1))


def _attn_kernel(q0, k0, v0, q1, k1, v1, q2, k2, v2, z_ref, o_ref, m_s, a_s, d_s, *perm):
    blk = ATTN_BLOCK
    qi = lax.broadcasted_iota(jnp.int32, (blk, blk), 0)
    ki = lax.broadcasted_iota(jnp.int32, (blk, blk), 1)
    mask_cur = ki <= qi
    mask_prev = ki >= qi
    nb = SEQ // blk
    qk_dims = (((2,), (2,)), ((0,), (0,)))
    pv_dims = (((2,), (1,)), ((0,), (0,)))

    for src, dst, dil in zip((q1, k1, v1, q2, k2, v2), perm, (DILATIONS[0],) * 3 + (DILATIONS[1],) * 3):
        sub = SEQ // dil
        cnt = PROJ_ROWS // dil
        for r in range(dil):
            for blk_i in range(SEQ // PROJ_ROWS):
                dst[r * sub + blk_i * cnt:r * sub + (blk_i + 1) * cnt, :] = (
                    src[blk_i * PROJ_ROWS + r * cnt:blk_i * PROJ_ROWS + (r + 1) * cnt, :])

    def group(q_ref, k_ref, v_ref, blocks_per_seq):
        q = q_ref[...].reshape(nb, blk, HEAD_DIM)
        k = k_ref[...].reshape(nb, blk, HEAD_DIM)
        v = jnp.concatenate([v_ref[...], jnp.ones((SEQ, LANES), BF16)], axis=1).reshape(nb, blk, -1)
        s_cur = lax.dot_general(q, k, qk_dims, preferred_element_type=F32)
        s_cur = jnp.where(mask_cur[None], s_cur, -jnp.inf)
        m = jnp.max(s_cur, axis=-1, keepdims=True)
        if blocks_per_seq > 1:
            k_prev = jnp.concatenate([k[:1], k[:-1]], axis=0)
            v_prev = jnp.concatenate([v[:1], v[:-1]], axis=0)
            b_idx = lax.broadcasted_iota(jnp.int32, (nb, 1, 1), 0)
            no_prev = jnp.where(b_idx % blocks_per_seq == 0, -jnp.inf, 0.0)
            s_prev = lax.dot_general(q, k_prev, qk_dims, preferred_element_type=F32)
            s_prev = jnp.where(mask_prev[None], s_prev, -jnp.inf) + no_prev
            m = jnp.maximum(m, jnp.max(s_prev, axis=-1, keepdims=True))
        p = jnp.exp2(s_cur - m)
        acc = lax.dot_general(p.astype(BF16), v, pv_dims, preferred_element_type=F32)
        if blocks_per_seq > 1:
            p = jnp.exp2(s_prev - m)
            acc = acc + lax.dot_general(p.astype(BF16), v_prev, pv_dims, preferred_element_type=F32)
        acc = acc.reshape(SEQ, HEAD_DIM + LANES)
        m = jnp.broadcast_to(m, (nb, blk, LANES)).reshape(SEQ, LANES)
        return m, acc[:, HEAD_DIM:], acc[:, :HEAD_DIM]

    for buf, val in zip((m_s, d_s, a_s), group(q0, k0, v0, nb)):
        buf[0] = val
    for g, dil in enumerate(DILATIONS):
        sub = SEQ // dil
        for buf, val in zip((m_s, d_s, a_s), group(*perm[3 * g:3 * g + 3], sub // blk)):
            for r in range(dil):
                buf[g + 1, pl.ds(r, sub, stride=dil), :] = val[r * sub:(r + 1) * sub, :]

    n_groups = len(DILATIONS) + 1
    m_all = [m_s[g] for g in range(n_groups)]
    m_max = functools.reduce(jnp.maximum, m_all)
    w = [jnp.exp2(m - m_max) for m in m_all]
    num = sum(w[g] * a_s[g] for g in range(n_groups))
    den = sum(w[g] * d_s[g] for g in range(n_groups))
    o_ref[...] = (num / den * z_ref[...].astype(F32)).astype(o_ref.dtype)


def _attention(p_main, dil_a, dil_b):
    hb = HEADS_PER_GROUP

    def spec(sec):
        return pl.BlockSpec((SEQ, HEAD_DIM), lambda b, h: (b, sec * hb + h))

    def main_spec(first):
        per = HALF_TILE // HEAD_DIM
        return pl.BlockSpec((SEQ, HEAD_DIM), lambda b, h: (b, first * per + h + per * (h // per)))

    return pl.pallas_call(
        _attn_kernel,
        grid=(BATCH, hb),
        in_specs=[main_spec(0), main_spec(4), main_spec(1), spec(0), spec(2), spec(0), spec(1), spec(2),
                  spec(1), main_spec(5)],
        out_specs=pl.BlockSpec((SEQ, HEAD_DIM), lambda b, h: (b, h)),
        out_shape=jax.ShapeDtypeStruct((TOKENS, GROUP_WIDTH), BF16),
        scratch_shapes=([pltpu.VMEM((len(DILATIONS) + 1, SEQ, LANES), F32)] * 3
                        + [pltpu.VMEM((SEQ, HEAD_DIM), BF16)] * 6),
        compiler_params=_params("arbitrary", "arbitrary"),
        name="dilated_attn",
    )(p_main, p_main, p_main, dil_a, dil_a, dil_b, dil_a, dil_b, dil_b, p_main)


def _ssd_kernel(xs_ref, b_ref, c_ref, z_ref, dt_ref, dtt_ref, arow_ref, acol_ref, dskip_ref, nw_ref,
                o_ref, rs_ref, state, cs3, cst_s, wrow_s, cdec_s):
    L = SSD_CHUNK
    G = SSD_GROUPS
    J = SSD_HEADS_PER_GROUP
    GW = SSD_GROUP_WIDTH
    N = SSD_STATE

    @pl.when(pl.program_id(1) == 0)
    def _():
        state[...] = jnp.zeros_like(state)

    li = lax.broadcasted_iota(jnp.int32, (L, L), 0)
    si = lax.broadcasted_iota(jnp.int32, (L, L), 1)
    tri = li >= si
    a_row = -jnp.exp(arow_ref[...])
    a_col = -jnp.exp(acol_ref[...])
    dtt = dtt_ref[...]
    cs = LOG2_E * jnp.dot(tri.astype(F32), dt_ref[...] * a_row, precision=HIGHEST,
                          preferred_element_type=F32)
    cst = LOG2_E * jnp.dot(dtt * a_col, (si >= li).astype(F32), precision=HIGHEST,
                           preferred_element_type=F32)
    cs_last = cst[:, L - 1:L]
    cst_s[...] = cst - jnp.log2(dtt)
    wrow_s[...] = dtt * jnp.exp2(cs_last - cst)
    cdec_s[...] = jnp.broadcast_to(jnp.exp2(cs_last), (SSD_HEADS, LANES))
    for g in range(G):
        cs3[g] = cs[:, g * J:(g + 1) * J]

    lane = lax.broadcasted_iota(jnp.int32, (L, LANES), 1)
    lo_half = lane < SSD_HEAD_DIM

    def split(v):
        zero = jnp.zeros_like(v)
        return jnp.where(lo_half, v, zero), jnp.where(lo_half, zero, v)

    def group(g):
        cols = slice(g * GW, (g + 1) * GW)
        ncols = slice(g * N, (g + 1) * N)
        heads = slice(g * J, (g + 1) * J)
        x_g = xs_ref[:, cols]
        b_g = b_ref[:, ncols]
        c_g = c_ref[:, ncols]
        cb = lax.dot_general(c_g, b_g, NT_DIMS, preferred_element_type=F32)
        bt = b_g.astype(F32).T
        cf = c_g.astype(F32)
        cs_g = cs3[g]
        cst_g = cst_s[heads, :]
        w_g = wrow_s[heads, :]
        cd_g = cdec_s[heads, :]
        st_prev = state[g]
        y_parts = []
        s_parts = []
        for p in range(J // 2):
            lhs_y = []
            lhs_s = []
            for j in (2 * p, 2 * p + 1):
                col = jnp.broadcast_to(cs_g[:, j:j + 1], (L, LANES))
                decay = jnp.exp2(jnp.where(tri, col - cst_g[j:j + 1, :], -jnp.inf))
                lhs_y.append((cb * decay).astype(BF16))
                lhs_y.append((cf * jnp.exp2(col)).astype(BF16))
                lhs_s.append((bt * w_g[j:j + 1, :]).astype(BF16))
            lanes_p = slice(p * LANES, (p + 1) * LANES)
            x_lo, x_hi = split(x_g[:, lanes_p])
            s_p = st_prev[:, lanes_p]
            s_lo, s_hi = split(s_p.astype(BF16))
            y_parts.append(jnp.dot(jnp.concatenate(lhs_y, axis=1),
                                   jnp.concatenate([x_lo, s_lo, x_hi, s_hi], axis=0),
                                   preferred_element_type=F32))
            cd = jnp.where(lo_half[0:1, :], cd_g[2 * p:2 * p + 1, :], cd_g[2 * p + 1:2 * p + 2, :])
            s_parts.append(s_p * cd + jnp.dot(jnp.concatenate(lhs_s, axis=1),
                                              jnp.concatenate([x_lo, x_hi], axis=0),
                                              preferred_element_type=F32))
        state[g] = jnp.concatenate(s_parts, axis=1)
        y = jnp.concatenate(y_parts, axis=1) + x_g.astype(F32) * dskip_ref[:, cols]
        y = y * z_ref[:, cols].astype(F32)
        o_ref[:, cols] = (y * nw_ref[:, cols]).astype(o_ref.dtype)
        return jnp.sum(y * y, axis=-1, keepdims=True)

    ssq = group(0)
    for g in range(1, G):
        ssq = ssq + group(g)
    rs_ref[...] = lax.rsqrt(ssq / SSD_INNER + NORM_EPS)


def _ssd(p_main, dt, dt_t, a_log, d_skip, norm_w):
    L = SSD_CHUNK
    row = lambda b, c: b * N_CHUNKS + c
    const = lambda b, c: (0, 0)
    return pl.pallas_call(
        _ssd_kernel,
        grid=(BATCH, N_CHUNKS),
        in_specs=[
            pl.BlockSpec((L, SSD_INNER), lambda b, c: (row(b, c), 2)),
            pl.BlockSpec((L, SSD_BC_WIDTH), lambda b, c: (row(b, c), 12)),
            pl.BlockSpec((L, SSD_BC_WIDTH), lambda b, c: (row(b, c), 13)),
            pl.BlockSpec((L, SSD_INNER), lambda b, c: (row(b, c), 1)),
            pl.BlockSpec((L, SSD_HEADS), lambda b, c: (row(b, c), 0)),
            pl.BlockSpec((SSD_HEADS, L), lambda b, c: (0, row(b, c))),
            pl.BlockSpec((1, SSD_HEADS), const),
            pl.BlockSpec((SSD_HEADS, 1), const),
            pl.BlockSpec((1, SSD_INNER), const),
            pl.BlockSpec((1, SSD_INNER), const),
        ],
        out_specs=[pl.BlockSpec((L, SSD_INNER), lambda b, c: (row(b, c), 0)),
                   pl.BlockSpec((L, 1), lambda b, c: (row(b, c), 0))],
        out_shape=[jax.ShapeDtypeStruct((TOKENS, SSD_INNER), BF16),
                   jax.ShapeDtypeStruct((TOKENS, 1), F32)],
        scratch_shapes=[
            pltpu.VMEM((SSD_GROUPS, SSD_STATE, SSD_GROUP_WIDTH), F32),
            pltpu.VMEM((SSD_GROUPS, L, SSD_HEADS_PER_GROUP), F32),
            pltpu.VMEM((SSD_HEADS, L), F32),
            pltpu.VMEM((SSD_HEADS, L), F32),
            pltpu.VMEM((SSD_HEADS, LANES), F32),
        ],
        compiler_params=_params("arbitrary", "arbitrary"),
        name="ssd",
    )(p_main, p_main, p_main, p_main, dt, dt_t,
      a_log.reshape(1, SSD_HEADS), a_log.reshape(SSD_HEADS, 1),
      jnp.repeat(d_skip, SSD_HEAD_DIM).reshape(1, SSD_INNER), norm_w.reshape(1, SSD_INNER))


def _branch_kernel(a_ref, s_ref, rs_ref, wa_ref, ws_ref, g0_ref, g1_ref, o_ref, wab, wsb):
    @pl.when(pl.program_id(1) == 0)
    def _():
        wab[...] = wa_ref[...].astype(BF16)
        wsb[...] = ws_ref[...].astype(BF16)

    for c in range(o_ref.shape[1] // SUB_COLS):
        cols = slice(c * SUB_COLS, (c + 1) * SUB_COLS)
        ya = jnp.dot(a_ref[...], wab[:, cols], preferred_element_type=F32)
        ys = jnp.dot(s_ref[...], wsb[:, cols], preferred_element_type=F32) * rs_ref[...]
        merged = g0_ref[:, cols].astype(F32) * ya + g1_ref[:, cols].astype(F32) * ys
        o_ref[:, cols] = merged.astype(o_ref.dtype)


def _branches(attn, ssd, ssd_rs, w_attn, w_ssd, gates):
    tn = 512
    nj = D_MODEL // tn
    return pl.pallas_call(
        _branch_kernel,
        grid=(nj, TOKENS // PROJ_ROWS),
        in_specs=[pl.BlockSpec((PROJ_ROWS, GROUP_WIDTH), lambda n, i: (i, 0)),
                  pl.BlockSpec((PROJ_ROWS, SSD_INNER), lambda n, i: (i, 0)),
                  pl.BlockSpec((PROJ_ROWS, 1), lambda n, i: (i, 0)),
                  pl.BlockSpec((None, GROUP_WIDTH, tn), lambda n, i: (0, 0, n)),
                  pl.BlockSpec((None, SSD_INNER, tn), lambda n, i: (0, 0, n)),
                  pl.BlockSpec((PROJ_ROWS, tn), lambda n, i: (i, n)),
                  pl.BlockSpec((PROJ_ROWS, tn), lambda n, i: (i, nj + n))],
        out_specs=pl.BlockSpec((PROJ_ROWS, tn), lambda n, i: (i, n)),
        out_shape=jax.ShapeDtypeStruct((TOKENS, D_MODEL), BF16),
        scratch_shapes=[pltpu.VMEM((GROUP_WIDTH, tn), BF16), pltpu.VMEM((SSD_INNER, tn), BF16)],
        compiler_params=_params("arbitrary", "arbitrary"),
        name="branches",
    )(attn, ssd, ssd_rs, w_attn, w_ssd, gates, gates)


def _out_kernel(m_ref, w_ref, x_ref, fw_ref, o_ref):
    sub = 2 * SUB_COLS
    ssq = jnp.zeros((o_ref.shape[0], 1), F32)
    for c in range(D_MODEL // sub):
        cols = slice(c * sub, (c + 1) * sub)
        h = x_ref[:, cols] + jnp.dot(m_ref[...], w_ref[:, cols], preferred_element_type=F32)
        ssq = ssq + jnp.sum(h * h, axis=-1, keepdims=True)
        o_ref[:, cols] = h
    o_ref[...] = o_ref[...] * lax.rsqrt(ssq / D_MODEL + NORM_EPS) * fw_ref[...]


def _out_proj(merged, w_out_bf, x2, final_w):
    tm = ROW_TILE
    return pl.pallas_call(
        _out_kernel,
        grid=(TOKENS // tm,),
        in_specs=[pl.BlockSpec((tm, D_MODEL), lambda i: (i, 0)),
                  pl.BlockSpec((D_MODEL, D_MODEL), lambda i: (0, 0)),
                  pl.BlockSpec((tm, D_MODEL), lambda i: (i, 0)),
                  pl.BlockSpec((1, D_MODEL), lambda i: (0, 0))],
        out_specs=pl.BlockSpec((tm, D_MODEL), lambda i: (i, 0)),
        out_shape=jax.ShapeDtypeStruct((TOKENS, D_MODEL), F32),
        compiler_params=_params("arbitrary"),
        name="out_proj",
    )(merged, w_out_bf, x2, final_w.reshape(1, D_MODEL))


def kernel(x, positions, norm_w, w_in, conv_w, conv_b, dt_bias, a_log, d_skip, ssd_norm_w,
           w_attn_br, w_ssd_br, gate_b, w_out, final_norm_w):
    assert x.shape == (BATCH, SEQ, D_MODEL) and w_in.shape[0] == 1
    x2 = x.reshape(TOKENS, D_MODEL)
    xn, cos, sin = _prep(x2, norm_w[0], positions)
    w_t = jnp.swapaxes(w_in, 1, 2)
    cw, cb = conv_w[0], conv_b.reshape(1, SSD_CONV_CH)
    p_main = _inproj(xn, w_t, cos, sin, cw, cb, _MAIN, "inproj_main")
    dil_a, dil_b = _inproj(xn, w_t, cos, sin, cw, cb, _DILATED, "inproj_dilated")
    gates, dt, dt_t = _gate_proj(xn, w_t, gate_b[0].reshape(1, 2 * D_MODEL), dt_bias[0])
    attn = _attention(p_main, dil_a, dil_b)
    ssd, ssd_rs = _ssd(p_main, dt, dt_t, a_log[0], d_skip[0], ssd_norm_w[0])
    merged = _branches(attn, ssd, ssd_rs, w_attn_br, w_ssd_br, gates)
    out = _out_proj(merged, w_out[0].astype(BF16), x2, final_norm_w)
    return out.reshape(BATCH, SEQ, D_MODEL)
```

```python
import functools

import jax
import jax.numpy as jnp
from jax import lax
from jax.experimental import pallas as pl
from jax.experimental.pallas import tpu as pltpu

F32 = jnp.float32
BF16 = jnp.bfloat16
HIGHEST = lax.Precision.HIGHEST

D_MODEL = 2048
BATCH = 4
SEQ = 2048
TOKENS = BATCH * SEQ
NORM_EPS = 1e-5

HEAD_DIM = 128
HEADS_PER_GROUP = 8
GROUP_WIDTH = HEADS_PER_GROUP * HEAD_DIM
ATTN_BLOCK = 128
DILATIONS = (4, 16)
ROPE_DIMS = HEAD_DIM // 4
ROPE_HALF = ROPE_DIMS // 2
ROPE_THETA = 500000.0
Q_SCALE = HEAD_DIM ** -0.5
LOG2_E = 1.4426950408889634

SSD_INNER = 2 * D_MODEL
SSD_HEAD_DIM = 64
SSD_HEADS = SSD_INNER // SSD_HEAD_DIM
SSD_GROUPS = 8
SSD_HEADS_PER_GROUP = SSD_HEADS // SSD_GROUPS
SSD_STATE = 128
SSD_CONV = 4
SSD_CHUNK = 128
SSD_GROUP_WIDTH = SSD_HEADS_PER_GROUP * SSD_HEAD_DIM
SSD_BC_WIDTH = SSD_GROUPS * SSD_STATE
SSD_CONV_CH = SSD_INNER + 2 * SSD_BC_WIDTH
N_CHUNKS = SEQ // SSD_CHUNK

COL_DT = 3 * 3 * GROUP_WIDTH + GROUP_WIDTH + SSD_INNER + SSD_CONV_CH
COL_GATES = COL_DT + SSD_HEADS

LANES = 128
ROW_TILE = 512
PROJ_ROWS = 1024
COL_TILE = 1024
HALF_TILE = COL_TILE // 2
SUB_COLS = 256
SUBS_PER_HALF = HALF_TILE // SUB_COLS
DEINT_STRIDE = 4
VMEM_LIMIT = 56 * 1024 * 1024


NT_DIMS = (((1,), (1,)), ((), ()))


def _params(*sem):
    return pltpu.CompilerParams(dimension_semantics=sem, vmem_limit_bytes=VMEM_LIMIT)


def _prep_kernel(x_ref, w_ref, pos_ref, invf_ref, xn_ref, cos_ref, sin_ref):
    x = x_ref[...]
    ms = jnp.mean(x * x, axis=-1, keepdims=True)
    xn_ref[...] = (x * lax.rsqrt(ms + NORM_EPS) * w_ref[...]).astype(xn_ref.dtype)
    ang = pos_ref[...].astype(F32) * invf_ref[...]
    lane = lax.broadcasted_iota(jnp.int32, ang.shape, 1)
    sign = jnp.where(lane < ROPE_HALF, -1.0, jnp.where(lane < ROPE_DIMS, 1.0, 0.0))
    cos_ref[...] = jnp.cos(ang)
    sin_ref[...] = jnp.sin(ang) * sign


def _prep(x2, w, positions):
    inv_freq = ROPE_THETA ** (-jnp.arange(ROPE_HALF, dtype=F32) / ROPE_HALF)
    invf = jnp.zeros((1, LANES), F32).at[0, :ROPE_DIMS].set(jnp.tile(inv_freq, 2))
    row_spec = lambda width: pl.BlockSpec((ROW_TILE, width), lambda i: (i, 0))
    const_spec = lambda width: pl.BlockSpec((1, width), lambda i: (0, 0))
    return pl.pallas_call(
        _prep_kernel,
        grid=(TOKENS // ROW_TILE,),
        in_specs=[row_spec(D_MODEL), const_spec(D_MODEL), row_spec(1), const_spec(LANES)],
        out_specs=[row_spec(D_MODEL), row_spec(LANES), row_spec(LANES)],
        out_shape=[jax.ShapeDtypeStruct((TOKENS, D_MODEL), BF16),
                   jax.ShapeDtypeStruct((TOKENS, LANES), F32),
                   jax.ShapeDtypeStruct((TOKENS, LANES), F32)],
        compiler_params=_params("arbitrary"),
        name="prep",
    )(x2, w.reshape(1, D_MODEL), positions.reshape(TOKENS, 1), invf)


def _silu(v):
    return 0.5 * v * (1.0 + jnp.tanh(0.5 * v))


def _lookup(n, values):
    out = values[-1]
    for k in range(len(values) - 2, -1, -1):
        out = jnp.where(n == k, values[k], out)
    return out


def _inproj_kernel(x_ref, wa_ref, wb_ref, cos_ref, sin_ref, cw_ref, cb_ref, *refs, plan, split_out,
                   row_split):
    if split_out:
        oa_ref, ob_ref, w_s, deint_ref, *stages = refs
    else:
        o_ref, w_s, deint_ref, *stages = refs
        oa_ref = o_ref.at[:, 0:HALF_TILE]
        ob_ref = o_ref.at[:, HALF_TILE:COL_TILE]
    n = pl.program_id(0)
    i = pl.program_id(1)

    @pl.when(i == 0)
    def _():
        w_s[0:HALF_TILE, :] = wa_ref[...].astype(BF16)
        w_s[HALF_TILE:COL_TILE, :] = wb_ref[...].astype(BF16)

    half_rows = PROJ_ROWS // row_split

    def sub_dot(c, rh):
        rows = slice(rh * half_rows, (rh + 1) * half_rows)
        return lax.dot_general(x_ref[rows, :], w_s[c * SUB_COLS:(c + 1) * SUB_COLS, :], NT_DIMS,
                               preferred_element_type=F32)

    top = stages[0].shape[1] - PROJ_ROWS
    slot0 = lax.shift_right_logical(i, 30)

    def emit(val, o_ref, col, c, h, dil, rh):
        rows = slice(rh * half_rows, (rh + 1) * half_rows)
        if dil == 1:
            o_ref[rows, col:col + LANES] = val.astype(o_ref.dtype)
            return
        slot = c * (SUB_COLS // LANES) + h
        slab = deint_ref.at[0, slot]
        slab[rows, :] = val
        if rh < row_split - 1:
            return
        groups = 1
        if dil != DEINT_STRIDE:
            assert dil == DEINT_STRIDE ** 2
            by_low = deint_ref.at[1, slot]
            cnt = PROJ_ROWS // DEINT_STRIDE
            for low in range(DEINT_STRIDE):
                by_low[low * cnt:(low + 1) * cnt, :] = slab[pl.ds(low, cnt, stride=DEINT_STRIDE), :]
            slab, groups = by_low, DEINT_STRIDE
        per_group = PROJ_ROWS // groups
        cnt = PROJ_ROWS // dil
        for low in range(groups):
            for high in range(DEINT_STRIDE):
                r = high * groups + low
                o_ref[r * cnt:(r + 1) * cnt, col:col + LANES] = (
                    slab[pl.ds(low * per_group + high, cnt, stride=DEINT_STRIDE), :].astype(o_ref.dtype))

    def epilogue(kind, c, rh, o_ref):
        kind, dil = kind if isinstance(kind, tuple) else (kind, 1)
        lc = c % SUBS_PER_HALF
        cols = slice(lc * SUB_COLS, (lc + 1) * SUB_COLS)
        rows = slice(rh * half_rows, (rh + 1) * half_rows)
        if kind == "plain" and dil > 1:
            acc = sub_dot(c, rh)
            for h in range(SUB_COLS // LANES):
                emit(acc[:, h * LANES:(h + 1) * LANES], o_ref, lc * SUB_COLS + h * LANES, c, h, dil, rh)
        elif kind == "plain":
            o_ref[rows, cols] = sub_dot(c, rh).astype(o_ref.dtype)
        elif kind == "silu":
            o_ref[rows, cols] = _silu(sub_dot(c, rh)).astype(o_ref.dtype)
        elif kind == "conv":
            st = stages[c % 2].at[c // 2]
            if rh == 0:
                first = (i % (SEQ // PROJ_ROWS)) == 0
                last = st[PROJ_ROWS:PROJ_ROWS + top, :]
                st[0:top, :] = jnp.where(first, 0.0, last)
            lo = top + rh * half_rows
            st[lo:lo + half_rows, :] = sub_dot(c, rh)
            rd = stages[c % 2].at[slot0 + c // 2]
            wcols = slice(c * SUB_COLS, (c + 1) * SUB_COLS)
            w = cw_ref[:, wcols]
            y = rd[lo:lo + half_rows, :] * w[SSD_CONV - 1:SSD_CONV, :] + cb_ref[:, wcols]
            for s in range(1, SSD_CONV):
                y = y + rd[lo - s:lo - s + half_rows, :] * w[SSD_CONV - 1 - s:SSD_CONV - s, :]
            o_ref[rows, cols] = _silu(y).astype(o_ref.dtype)
        else:
            acc = sub_dot(c, rh)
            cos = cos_ref[rows, :]
            sin = sin_ref[rows, :]
            scale = Q_SCALE * LOG2_E if kind == "rope_q" else 1.0
            lane = lax.broadcasted_iota(jnp.int32, (half_rows, LANES), 1)
            for h in range(SUB_COLS // HEAD_DIM):
                t = acc[:, h * HEAD_DIM:(h + 1) * HEAD_DIM]
                partner = jnp.where(lane < ROPE_HALF,
                                    pltpu.roll(t, LANES - ROPE_HALF, 1),
                                    pltpu.roll(t, ROPE_HALF, 1))
                emit((t * cos + partner * sin) * scale, o_ref, lc * SUB_COLS + h * HEAD_DIM, c, h, dil, rh)

    for lo_step, hi_step, kind_a, kind_b in plan:
        @pl.when((n >= lo_step) & (n < hi_step))
        def _(kind_a=kind_a, kind_b=kind_b):
            for rh in range(row_split):
                for c in range(SUBS_PER_HALF):
                    epilogue(kind_a, c, rh, oa_ref)
                for c in range(SUBS_PER_HALF, 2 * SUBS_PER_HALF):
                    epilogue(kind_b, c, rh, ob_ref)


_W_Q, _W_K, _W_V = 0, 6, 12
_W_ZATTN, _W_ZSSD, _W_XBC = 18, 20, 28
_MAIN = dict(
    wa=[_W_Q, _W_Q + 1, _W_K, _W_K + 1] + [_W_ZSSD + 2 * k for k in range(4)]
    + [_W_XBC + 2 * k for k in range(6)],
    wb=[_W_V, _W_V + 1, _W_ZATTN, _W_ZATTN + 1] + [_W_ZSSD + 2 * k + 1 for k in range(4)]
    + [_W_XBC + 2 * k + 1 for k in range(6)],
    plan=((0, 2, "rope_q", "plain"), (2, 4, "rope_k", "silu"), (4, 8, "silu", "silu"),
          (8, 14, "conv", "conv")),
    conv_first=8, split_out=False, row_split=1, dtype=BF16)
_DILATED = dict(
    wa=[_W_Q + 2, _W_Q + 3, _W_Q + 4, _W_Q + 5, _W_K + 2, _W_K + 3],
    wb=[_W_V + 2, _W_V + 3, _W_V + 4, _W_V + 5, _W_K + 4, _W_K + 5],
    plan=((0, 2, ("rope_q", 4), ("plain", 4)), (2, 4, ("rope_q", 16), ("plain", 16)),
          (4, 6, ("rope_k", 4), ("rope_k", 16))),
    conv_first=0, split_out=True, row_split=2, dtype=BF16)


def _inproj(xn, w_t, cos, sin, conv_w, conv_b, cfg, name):
    steps = len(cfg["wa"])
    t_spec = pl.BlockSpec((PROJ_ROWS, LANES), lambda n, i: (i, 0))
    conv_col = lambda n, i: (0, jnp.clip(n - cfg["conv_first"], 0, SSD_CONV_CH // COL_TILE - 1))
    w_spec = lambda rows: pl.BlockSpec((None, HALF_TILE, D_MODEL), lambda n, i: (0, _lookup(n, rows), 0))
    if cfg["split_out"]:
        out_specs = [pl.BlockSpec((PROJ_ROWS, HALF_TILE), lambda n, i: (i, n))] * 2
        out_shape = [jax.ShapeDtypeStruct((TOKENS, steps * HALF_TILE), cfg["dtype"])] * 2
    else:
        out_specs = pl.BlockSpec((PROJ_ROWS, COL_TILE), lambda n, i: (i, n))
        out_shape = jax.ShapeDtypeStruct((TOKENS, steps * COL_TILE), cfg["dtype"])
    stage = pltpu.VMEM((2, 8 + PROJ_ROWS, SUB_COLS), F32)
    return pl.pallas_call(
        functools.partial(_inproj_kernel, plan=cfg["plan"], split_out=cfg["split_out"],
                          row_split=cfg["row_split"]),
        grid=(steps, TOKENS // PROJ_ROWS),
        in_specs=[pl.BlockSpec((PROJ_ROWS, D_MODEL), lambda n, i: (i, 0)),
                  w_spec(cfg["wa"]), w_spec(cfg["wb"]), t_spec, t_spec,
                  pl.BlockSpec((SSD_CONV, COL_TILE), conv_col),
                  pl.BlockSpec((1, COL_TILE), conv_col)],
        out_specs=out_specs,
        out_shape=out_shape,
        scratch_shapes=[pltpu.VMEM((COL_TILE, D_MODEL), BF16),
                        pltpu.VMEM((2, COL_TILE // LANES, PROJ_ROWS, LANES), F32)]
        + [stage] * 2,
        compiler_params=_params("arbitrary", "arbitrary"),
        name=name,
    )(xn, w_t, w_t, cos, sin, conv_w, conv_b)


def _gate_kernel(x_ref, wa_ref, wb_ref, b_ref, dtb_col_ref, o_ref, dt_ref, dtt_ref, w_s):
    @pl.when(pl.program_id(1) == 0)
    def _():
        w_s[0:COL_TILE - SSD_HEADS, :] = wa_ref[SSD_HEADS:COL_TILE, :].astype(BF16)
        w_s[COL_TILE - SSD_HEADS:COL_TILE, :] = wb_ref[...].astype(BF16)

    @pl.when(pl.program_id(0) == 0)
    def _():
        w = wa_ref[0:SSD_HEADS, :].astype(BF16)
        dtt = jax.nn.softplus(
            lax.dot_general(w, x_ref[...], NT_DIMS, preferred_element_type=F32) + dtb_col_ref[...])
        dtt_ref[...] = dtt
        dt_ref[...] = dtt.T

    for c in range(COL_TILE // SUB_COLS):
        cols = slice(c * SUB_COLS, (c + 1) * SUB_COLS)
        acc = lax.dot_general(x_ref[...], w_s[cols, :], NT_DIMS, preferred_element_type=F32)
        logits = acc + b_ref[:, cols]
        o_ref[:, cols] = (0.5 + 0.5 * jnp.tanh(0.5 * logits)).astype(o_ref.dtype)


def _gate_proj(xn, w_t, gate_bias, dt_bias):
    n_out = 2 * D_MODEL
    first = COL_DT // COL_TILE
    per = COL_TILE // SSD_HEADS
    n_rows = TOKENS // PROJ_ROWS
    dt_row = lambda n, i: jnp.where(n == 0, i, n_rows - 1)
    return pl.pallas_call(
        _gate_kernel,
        grid=(n_out // COL_TILE, n_rows),
        in_specs=[pl.BlockSpec((PROJ_ROWS, D_MODEL), lambda n, i: (i, 0)),
                  pl.BlockSpec((None, COL_TILE, D_MODEL), lambda n, i: (0, first + n, 0)),
                  pl.BlockSpec((None, SSD_HEADS, D_MODEL), lambda n, i: (0, (first + n + 1) * per, 0)),
                  pl.BlockSpec((1, COL_TILE), lambda n, i: (0, n)),
                  pl.BlockSpec((SSD_HEADS, 1), lambda n, i: (0, 0))],
        out_specs=[pl.BlockSpec((PROJ_ROWS, COL_TILE), lambda n, i: (i, n)),
                   pl.BlockSpec((PROJ_ROWS, SSD_HEADS), lambda n, i: (dt_row(n, i), 0)),
                   pl.BlockSpec((SSD_HEADS, PROJ_ROWS), lambda n, i: (0, dt_row(n, i)))],
        out_shape=[jax.ShapeDtypeStruct((TOKENS, n_out), BF16),
                   jax.ShapeDtypeStruct((TOKENS, SSD_HEADS), F32),
                   jax.ShapeDtypeStruct((SSD_HEADS, TOKENS), F32)],
        scratch_shapes=[pltpu.VMEM((COL_TILE, D_MODEL), BF16)],
        compiler_params=_params("arbitrary", "arbitrary"),
        name="gate_proj",
    )(xn, w_t, w_t, gate_bias, dt_bias.reshape(SSD_HEADS, 1))


def _attn_kernel(q0, k0, v0, q1, k1, v1, q2, k2, v2, z_ref, o_ref, m_s, a_s, d_s, *perm):
    blk = ATTN_BLOCK
    qi = lax.broadcasted_iota(jnp.int32, (blk, blk), 0)
    ki = lax.broadcasted_iota(jnp.int32, (blk, blk), 1)
    mask_cur = ki <= qi
    mask_prev = ki >= qi
    nb = SEQ // blk
    qk_dims = (((2,), (2,)), ((0,), (0,)))
    pv_dims = (((2,), (1,)), ((0,), (0,)))

    for src, dst, dil in zip((q1, k1, v1, q2, k2, v2), perm, (DILATIONS[0],) * 3 + (DILATIONS[1],) * 3):
        sub = SEQ // dil
        cnt = PROJ_ROWS // dil
        for r in range(dil):
            for blk_i in range(SEQ // PROJ_ROWS):
                dst[r * sub + blk_i * cnt:r * sub + (blk_i + 1) * cnt, :] = (
                    src[blk_i * PROJ_ROWS + r * cnt:blk_i * PROJ_ROWS + (r + 1) * cnt, :])

    def group(q_ref, k_ref, v_ref, blocks_per_seq):
        q = q_ref[...].reshape(nb, blk, HEAD_DIM)
        k = k_ref[...].reshape(nb, blk, HEAD_DIM)
        v = jnp.concatenate([v_ref[...], jnp.ones((SEQ, LANES), BF16)], axis=1).reshape(nb, blk, -1)
        s_cur = lax.dot_general(q, k, qk_dims, preferred_element_type=F32)
        s_cur = jnp.where(mask_cur[None], s_cur, -jnp.inf)
        m = jnp.max(s_cur, axis=-1, keepdims=True)
        if blocks_per_seq > 1:
            k_prev = jnp.concatenate([k[:1], k[:-1]], axis=0)
            v_prev = jnp.concatenate([v[:1], v[:-1]], axis=0)
            b_idx = lax.broadcasted_iota(jnp.int32, (nb, 1, 1), 0)
            no_prev = jnp.where(b_idx % blocks_per_seq == 0, -jnp.inf, 0.0)
            s_prev = lax.dot_general(q, k_prev, qk_dims, preferred_element_type=F32)
            s_prev = jnp.where(mask_prev[None], s_prev, -jnp.inf) + no_prev
            m = jnp.maximum(m, jnp.max(s_prev, axis=-1, keepdims=True))
        p = jnp.exp2(s_cur - m)
        acc = lax.dot_general(p.astype(BF16), v, pv_dims, preferred_element_type=F32)
        if blocks_per_seq > 1:
            p = jnp.exp2(s_prev - m)
            acc = acc + lax.dot_general(p.astype(BF16), v_prev, pv_dims, preferred_element_type=F32)
        acc = acc.reshape(SEQ, HEAD_DIM + LANES)
        m = jnp.broadcast_to(m, (nb, blk, LANES)).reshape(SEQ, LANES)
        return m, acc[:, HEAD_DIM:], acc[:, :HEAD_DIM]

    for buf, val in zip((m_s, d_s, a_s), group(q0, k0, v0, nb)):
        buf[0] = val
    for g, dil in enumerate(DILATIONS):
        sub = SEQ // dil
        for buf, val in zip((m_s, d_s, a_s), group(*perm[3 * g:3 * g + 3], sub // blk)):
            for r in range(dil):
                buf[g + 1, pl.ds(r, sub, stride=dil), :] = val[r * sub:(r + 1) * sub, :]

    n_groups = len(DILATIONS) + 1
    m_all = [m_s[g] for g in range(n_groups)]
    m_max = functools.reduce(jnp.maximum, m_all)
    w = [jnp.exp2(m - m_max) for m in m_all]
    num = sum(w[g] * a_s[g] for g in range(n_groups))
    den = sum(w[g] * d_s[g] for g in range(n_groups))
    o_ref[...] = (num / den * z_ref[...].astype(F32)).astype(o_ref.dtype)


def _attention(p_main, dil_a, dil_b):
    hb = HEADS_PER_GROUP

    def spec(sec):
        return pl.BlockSpec((SEQ, HEAD_DIM), lambda b, h: (b, sec * hb + h))

    def main_spec(first):
        per = HALF_TILE // HEAD_DIM
        return pl.BlockSpec((SEQ, HEAD_DIM), lambda b, h: (b, first * per + h + per * (h // per)))

    return pl.pallas_call(
        _attn_kernel,
        grid=(BATCH, hb),
        in_specs=[main_spec(0), main_spec(4), main_spec(1), spec(0), spec(2), spec(0), spec(1), spec(2),
                  spec(1), main_spec(5)],
        out_specs=pl.BlockSpec((SEQ, HEAD_DIM), lambda b, h: (b, h)),
        out_shape=jax.ShapeDtypeStruct((TOKENS, GROUP_WIDTH), BF16),
        scratch_shapes=([pltpu.VMEM((len(DILATIONS) + 1, SEQ, LANES), F32)] * 3
                        + [pltpu.VMEM((SEQ, HEAD_DIM), BF16)] * 6),
        compiler_params=_params("arbitrary", "arbitrary"),
        name="dilated_attn",
    )(p_main, p_main, p_main, dil_a, dil_a, dil_b, dil_a, dil_b, dil_b, p_main)


def _ssd_kernel(xs_ref, b_ref, c_ref, z_ref, dt_ref, dtt_ref, arow_ref, acol_ref, dskip_ref, nw_ref,
                o_ref, rs_ref, state, cs3, cst_s, wrow_s, cdec_s):
    L = SSD_CHUNK
    G = SSD_GROUPS
    J = SSD_HEADS_PER_GROUP
    GW = SSD_GROUP_WIDTH
    N = SSD_STATE

    @pl.when(pl.program_id(1) == 0)
    def _():
        state[...] = jnp.zeros_like(state)

    li = lax.broadcasted_iota(jnp.int32, (L, L), 0)
    si = lax.broadcasted_iota(jnp.int32, (L, L), 1)
    tri = li >= si
    a_row = -jnp.exp(arow_ref[...])
    a_col = -jnp.exp(acol_ref[...])
    dtt = dtt_ref[...]
    cs = LOG2_E * jnp.dot(tri.astype(F32), dt_ref[...] * a_row, precision=HIGHEST,
                          preferred_element_type=F32)
    cst = LOG2_E * jnp.dot(dtt * a_col, (si >= li).astype(F32), precision=HIGHEST,
                           preferred_element_type=F32)
    cs_last = cst[:, L - 1:L]
    cst_s[...] = cst - jnp.log2(dtt)
    wrow_s[...] = dtt * jnp.exp2(cs_last - cst)
    cdec_s[...] = jnp.broadcast_to(jnp.exp2(cs_last), (SSD_HEADS, LANES))
    for g in range(G):
        cs3[g] = cs[:, g * J:(g + 1) * J]

    lane = lax.broadcasted_iota(jnp.int32, (L, LANES), 1)
    lo_half = lane < SSD_HEAD_DIM

    def split(v):
        zero = jnp.zeros_like(v)
        return jnp.where(lo_half, v, zero), jnp.where(lo_half, zero, v)

    def group(g):
        cols = slice(g * GW, (g + 1) * GW)
        ncols = slice(g * N, (g + 1) * N)
        heads = slice(g * J, (g + 1) * J)
        x_g = xs_ref[:, cols]
        b_g = b_ref[:, ncols]
        c_g = c_ref[:, ncols]
        cb = lax.dot_general(c_g, b_g, NT_DIMS, preferred_element_type=F32)
        bt = b_g.astype(F32).T
        cf = c_g.astype(F32)
        cs_g = cs3[g]
        cst_g = cst_s[heads, :]
        w_g = wrow_s[heads, :]
        cd_g = cdec_s[heads, :]
        st_prev = state[g]
        y_parts = []
        s_parts = []
        for p in range(J // 2):
            lhs_y = []
            lhs_s = []
            for j in (2 * p, 2 * p + 1):
                col = jnp.broadcast_to(cs_g[:, j:j + 1], (L, LANES))
                decay = jnp.exp2(jnp.where(tri, col - cst_g[j:j + 1, :], -jnp.inf))
                lhs_y.append((cb * decay).astype(BF16))
                lhs_y.append((cf * jnp.exp2(col)).astype(BF16))
                lhs_s.append((bt * w_g[j:j + 1, :]).astype(BF16))
            lanes_p = slice(p * LANES, (p + 1) * LANES)
            x_lo, x_hi = split(x_g[:, lanes_p])
            s_p = st_prev[:, lanes_p]
            s_lo, s_hi = split(s_p.astype(BF16))
            y_parts.append(jnp.dot(jnp.concatenate(lhs_y, axis=1),
                                   jnp.concatenate([x_lo, s_lo, x_hi, s_hi], axis=0),
                                   preferred_element_type=F32))
            cd = jnp.where(lo_half[0:1, :], cd_g[2 * p:2 * p + 1, :], cd_g[2 * p + 1:2 * p + 2, :])
            s_parts.append(s_p * cd + jnp.dot(jnp.concatenate(lhs_s, axis=1),
                                              jnp.concatenate([x_lo, x_hi], axis=0),
                                              preferred_element_type=F32))
        state[g] = jnp.concatenate(s_parts, axis=1)
        y = jnp.concatenate(y_parts, axis=1) + x_g.astype(F32) * dskip_ref[:, cols]
        y = y * z_ref[:, cols].astype(F32)
        o_ref[:, cols] = (y * nw_ref[:, cols]).astype(o_ref.dtype)
        return jnp.sum(y * y, axis=-1, keepdims=True)

    ssq = group(0)
    for g in range(1, G):
        ssq = ssq + group(g)
    rs_ref[...] = lax.rsqrt(ssq / SSD_INNER + NORM_EPS)


def _ssd(p_main, dt, dt_t, a_log, d_skip, norm_w):
    L = SSD_CHUNK
    row = lambda b, c: b * N_CHUNKS + c
    const = lambda b, c: (0, 0)
    return pl.pallas_call(
        _ssd_kernel,
        grid=(BATCH, N_CHUNKS),
        in_specs=[
            pl.BlockSpec((L, SSD_INNER), lambda b, c: (row(b, c), 2)),
            pl.BlockSpec((L, SSD_BC_WIDTH), lambda b, c: (row(b, c), 12)),
            pl.BlockSpec((L, SSD_BC_WIDTH), lambda b, c: (row(b, c), 13)),
            pl.BlockSpec((L, SSD_INNER), lambda b, c: (row(b, c), 1)),
            pl.BlockSpec((L, SSD_HEADS), lambda b, c: (row(b, c), 0)),
            pl.BlockSpec((SSD_HEADS, L), lambda b, c: (0, row(b, c))),
            pl.BlockSpec((1, SSD_HEADS), const),
            pl.BlockSpec((SSD_HEADS, 1), const),
            pl.BlockSpec((1, SSD_INNER), const),
            pl.BlockSpec((1, SSD_INNER), const),
        ],
        out_specs=[pl.BlockSpec((L, SSD_INNER), lambda b, c: (row(b, c), 0)),
                   pl.BlockSpec((L, 1), lambda b, c: (row(b, c), 0))],
        out_shape=[jax.ShapeDtypeStruct((TOKENS, SSD_INNER), BF16),
                   jax.ShapeDtypeStruct((TOKENS, 1), F32)],
        scratch_shapes=[
            pltpu.VMEM((SSD_GROUPS, SSD_STATE, SSD_GROUP_WIDTH), F32),
            pltpu.VMEM((SSD_GROUPS, L, SSD_HEADS_PER_GROUP), F32),
            pltpu.VMEM((SSD_HEADS, L), F32),
            pltpu.VMEM((SSD_HEADS, L), F32),
            pltpu.VMEM((SSD_HEADS, LANES), F32),
        ],
        compiler_params=_params("arbitrary", "arbitrary"),
        name="ssd",
    )(p_main, p_main, p_main, p_main, dt, dt_t,
      a_log.reshape(1, SSD_HEADS), a_log.reshape(SSD_HEADS, 1),
      jnp.repeat(d_skip, SSD_HEAD_DIM).reshape(1, SSD_INNER), norm_w.reshape(1, SSD_INNER))


def _branch_kernel(a_ref, s_ref, rs_ref, wa_ref, ws_ref, g0_ref, g1_ref, o_ref, wab, wsb):
    @pl.when(pl.program_id(1) == 0)
    def _():
        wab[...] = wa_ref[...].astype(BF16)
        wsb[...] = ws_ref[...].astype(BF16)

    for c in range(o_ref.shape[1] // SUB_COLS):
        cols = slice(c * SUB_COLS, (c + 1) * SUB_COLS)
        ya = jnp.dot(a_ref[...], wab[:, cols], preferred_element_type=F32)
        ys = jnp.dot(s_ref[...], wsb[:, cols], preferred_element_type=F32) * rs_ref[...]
        merged = g0_ref[:, cols].astype(F32) * ya + g1_ref[:, cols].astype(F32) * ys
        o_ref[:, cols] = merged.astype(o_ref.dtype)


def _branches(attn, ssd, ssd_rs, w_attn, w_ssd, gates):
    tn = 512
    nj = D_MODEL // tn
    return pl.pallas_call(
        _branch_kernel,
        grid=(nj, TOKENS // PROJ_ROWS),
        in_specs=[pl.BlockSpec((PROJ_ROWS, GROUP_WIDTH), lambda n, i: (i, 0)),
                  pl.BlockSpec((PROJ_ROWS, SSD_INNER), lambda n, i: (i, 0)),
                  pl.BlockSpec((PROJ_ROWS, 1), lambda n, i: (i, 0)),
                  pl.BlockSpec((None, GROUP_WIDTH, tn), lambda n, i: (0, 0, n)),
                  pl.BlockSpec((None, SSD_INNER, tn), lambda n, i: (0, 0, n)),
                  pl.BlockSpec((PROJ_ROWS, tn), lambda n, i: (i, n)),
                  pl.BlockSpec((PROJ_ROWS, tn), lambda n, i: (i, nj + n))],
        out_specs=pl.BlockSpec((PROJ_ROWS, tn), lambda n, i: (i, n)),
        out_shape=jax.ShapeDtypeStruct((TOKENS, D_MODEL), BF16),
        scratch_shapes=[pltpu.VMEM((GROUP_WIDTH, tn), BF16), pltpu.VMEM((SSD_INNER, tn), BF16)],
        compiler_params=_params("arbitrary", "arbitrary"),
        name="branches",
    )(attn, ssd, ssd_rs, w_attn, w_ssd, gates, gates)


def _out_kernel(m_ref, w_ref, x_ref, fw_ref, o_ref):
    sub = 2 * SUB_COLS
    ssq = jnp.zeros((o_ref.shape[0], 1), F32)
    for c in range(D_MODEL // sub):
        cols = slice(c * sub, (c + 1) * sub)
        h = x_ref[:, cols] + jnp.dot(m_ref[...], w_ref[:, cols], preferred_element_type=F32)
        ssq = ssq + jnp.sum(h * h, axis=-1, keepdims=True)
        o_ref[:, cols] = h
    o_ref[...] = o_ref[...] * lax.rsqrt(ssq / D_MODEL + NORM_EPS) * fw_ref[...]


def _out_proj(merged, w_out_bf, x2, final_w):
    tm = ROW_TILE
    return pl.pallas_call(
        _out_kernel,
        grid=(TOKENS // tm,),
        in_specs=[pl.BlockSpec((tm, D_MODEL), lambda i: (i, 0)),
                  pl.BlockSpec((D_MODEL, D_MODEL), lambda i: (0, 0)),
                  pl.BlockSpec((tm, D_MODEL), lambda i: (i, 0)),
                  pl.BlockSpec((1, D_MODEL), lambda i: (0, 0))],
        out_specs=pl.BlockSpec((tm, D_MODEL), lambda i: (i, 0)),
        out_shape=jax.ShapeDtypeStruct((TOKENS, D_MODEL), F32),
        compiler_params=_params("arbitrary"),
        name="out_proj",
    )(merged, w_out_bf, x2, final_w.reshape(1, D_MODEL))


def kernel(x, positions, norm_w, w_in, conv_w, conv_b, dt_bias, a_log, d_skip, ssd_norm_w,
           w_attn_br, w_ssd_br, gate_b, w_out, final_norm_w):
    assert x.shape == (BATCH, SEQ, D_MODEL) and w_in.shape[0] == 1
    x2 = x.reshape(TOKENS, D_MODEL)
    xn, cos, sin = _prep(x2, norm_w[0], positions)
    w_t = jnp.swapaxes(w_in, 1, 2)
    cw, cb = conv_w[0], conv_b.reshape(1, SSD_CONV_CH)
    p_main = _inproj(xn, w_t, cos, sin, cw, cb, _MAIN, "inproj_main")
    dil_a, dil_b = _inproj(xn, w_t, cos, sin, cw, cb, _DILATED, "inproj_dilated")
    gates, dt, dt_t = _gate_proj(xn, w_t, gate_b[0].reshape(1, 2 * D_MODEL), dt_bias[0])
    attn = _attention(p_main, dil_a, dil_b)
    ssd, ssd_rs = _ssd(p_main, dt, dt_t, a_log[0], d_skip[0], ssd_norm_w[0])
    merged = _branches(attn, ssd, ssd_rs, w_attn_br, w_ssd_br, gates)
    out = _out_proj(merged, w_out[0].astype(BF16), x2, final_norm_w)
    return out.reshape(BATCH, SEQ, D_MODEL)
```

```python
import functools

import jax
import jax.numpy as jnp
from jax import lax
from jax.experimental import pallas as pl
from jax.experimental.pallas import tpu as pltpu

F32 = jnp.float32
BF16 = jnp.bfloat16
HIGHEST = lax.Precision.HIGHEST

D_MODEL = 2048
BATCH = 4
SEQ = 2048
TOKENS = BATCH * SEQ
NORM_EPS = 1e-5

HEAD_DIM = 128
HEADS_PER_GROUP = 8
GROUP_WIDTH = HEADS_PER_GROUP * HEAD_DIM
ATTN_BLOCK = 128
DILATIONS = (4, 16)
ROPE_DIMS = HEAD_DIM // 4
ROPE_HALF = ROPE_DIMS // 2
ROPE_THETA = 500000.0
Q_SCALE = HEAD_DIM ** -0.5
LOG2_E = 1.4426950408889634

SSD_INNER = 2 * D_MODEL
SSD_HEAD_DIM = 64
SSD_HEADS = SSD_INNER // SSD_HEAD_DIM
SSD_GROUPS = 8
SSD_HEADS_PER_GROUP = SSD_HEADS // SSD_GROUPS
SSD_STATE = 128
SSD_CONV = 4
SSD_CHUNK = 128
SSD_GROUP_WIDTH = SSD_HEADS_PER_GROUP * SSD_HEAD_DIM
SSD_BC_WIDTH = SSD_GROUPS * SSD_STATE
SSD_CONV_CH = SSD_INNER + 2 * SSD_BC_WIDTH
N_CHUNKS = SEQ // SSD_CHUNK

COL_DT = 3 * 3 * GROUP_WIDTH + GROUP_WIDTH + SSD_INNER + SSD_CONV_CH
COL_GATES = COL_DT + SSD_HEADS

LANES = 128
ROW_TILE = 512
PROJ_ROWS = 1024
COL_TILE = 1024
HALF_TILE = COL_TILE // 2
SUB_COLS = 256
SUBS_PER_HALF = HALF_TILE // SUB_COLS
DEINT_STRIDE = 4
VMEM_LIMIT = 56 * 1024 * 1024


NT_DIMS = (((1,), (1,)), ((), ()))


def _params(*sem):
    return pltpu.CompilerParams(dimension_semantics=sem, vmem_limit_bytes=VMEM_LIMIT)


def _prep_kernel(x_ref, w_ref, pos_ref, invf_ref, xn_ref, cos_ref, sin_ref):
    x = x_ref[...]
    ms = jnp.mean(x * x, axis=-1, keepdims=True)
    xn_ref[...] = (x * lax.rsqrt(ms + NORM_EPS) * w_ref[...]).astype(xn_ref.dtype)
    ang = pos_ref[...].astype(F32) * invf_ref[...]
    lane = lax.broadcasted_iota(jnp.int32, ang.shape, 1)
    sign = jnp.where(lane < ROPE_HALF, -1.0, jnp.where(lane < ROPE_DIMS, 1.0, 0.0))
    cos_ref[...] = jnp.cos(ang)
    sin_ref[...] = jnp.sin(ang) * sign


def _prep(x2, w, positions):
    inv_freq = ROPE_THETA ** (-jnp.arange(ROPE_HALF, dtype=F32) / ROPE_HALF)
    invf = jnp.zeros((1, LANES), F32).at[0, :ROPE_DIMS].set(jnp.tile(inv_freq, 2))
    row_spec = lambda width: pl.BlockSpec((ROW_TILE, width), lambda i: (i, 0))
    const_spec = lambda width: pl.BlockSpec((1, width), lambda i: (0, 0))
    return pl.pallas_call(
        _prep_kernel,
        grid=(TOKENS // ROW_TILE,),
        in_specs=[row_spec(D_MODEL), const_spec(D_MODEL), row_spec(1), const_spec(LANES)],
        out_specs=[row_spec(D_MODEL), row_spec(LANES), row_spec(LANES)],
        out_shape=[jax.ShapeDtypeStruct((TOKENS, D_MODEL), BF16),
                   jax.ShapeDtypeStruct((TOKENS, LANES), F32),
                   jax.ShapeDtypeStruct((TOKENS, LANES), F32)],
        compiler_params=_params("arbitrary"),
        name="prep",
    )(x2, w.reshape(1, D_MODEL), positions.reshape(TOKENS, 1), invf)


def _silu(v):
    return 0.5 * v * (1.0 + jnp.tanh(0.5 * v))


def _lookup(n, values):
    out = values[-1]
    for k in range(len(values) - 2, -1, -1):
        out = jnp.where(n == k, values[k], out)
    return out


def _inproj_kernel(x_ref, wa_ref, wb_ref, cos_ref, sin_ref, cw_ref, cb_ref, *refs, plan, split_out,
                   row_split):
    if split_out:
        oa_ref, ob_ref, w_s, deint_ref, *stages = refs
    else:
        o_ref, w_s, deint_ref, *stages = refs
        oa_ref = o_ref.at[:, 0:HALF_TILE]
        ob_ref = o_ref.at[:, HALF_TILE:COL_TILE]
    n = pl.program_id(0)
    i = pl.program_id(1)

    @pl.when(i == 0)
    def _():
        w_s[0:HALF_TILE, :] = wa_ref[...].astype(BF16)
        w_s[HALF_TILE:COL_TILE, :] = wb_ref[...].astype(BF16)

    half_rows = PROJ_ROWS // row_split

    def sub_dot(c, rh):
        rows = slice(rh * half_rows, (rh + 1) * half_rows)
        return lax.dot_general(x_ref[rows, :], w_s[c * SUB_COLS:(c + 1) * SUB_COLS, :], NT_DIMS,
                               preferred_element_type=F32)

    top = stages[0].shape[1] - PROJ_ROWS
    slot0 = lax.shift_right_logical(i, 30)

    def emit(val, o_ref, col, c, h, dil, rh):
        rows = slice(rh * half_rows, (rh + 1) * half_rows)
        if dil == 1:
            o_ref[rows, col:col + LANES] = val.astype(o_ref.dtype)
            return
        slot = c * (SUB_COLS // LANES) + h
        slab = deint_ref.at[0, slot]
        slab[rows, :] = val
        if rh < row_split - 1:
            return
        groups = 1
        if dil != DEINT_STRIDE:
            assert dil == DEINT_STRIDE ** 2
            by_low = deint_ref.at[1, slot]
            cnt = PROJ_ROWS // DEINT_STRIDE
            for low in range(DEINT_STRIDE):
                by_low[low * cnt:(low + 1) * cnt, :] = slab[pl.ds(low, cnt, stride=DEINT_STRIDE), :]
            slab, groups = by_low, DEINT_STRIDE
        per_group = PROJ_ROWS // groups
        cnt = PROJ_ROWS // dil
        for low in range(groups):
            for high in range(DEINT_STRIDE):
                r = high * groups + low
                o_ref[r * cnt:(r + 1) * cnt, col:col + LANES] = (
                    slab[pl.ds(low * per_group + high, cnt, stride=DEINT_STRIDE), :].astype(o_ref.dtype))

    def epilogue(kind, c, rh, o_ref):
        kind, dil = kind if isinstance(kind, tuple) else (kind, 1)
        lc = c % SUBS_PER_HALF
        cols = slice(lc * SUB_COLS, (lc + 1) * SUB_COLS)
        rows = slice(rh * half_rows, (rh + 1) * half_rows)
        if kind == "plain" and dil > 1:
            acc = sub_dot(c, rh)
            for h in range(SUB_COLS // LANES):
                emit(acc[:, h * LANES:(h + 1) * LANES], o_ref, lc * SUB_COLS + h * LANES, c, h, dil, rh)
        elif kind == "plain":
            o_ref[rows, cols] = sub_dot(c, rh).astype(o_ref.dtype)
        elif kind == "silu":
            o_ref[rows, cols] = _silu(sub_dot(c, rh)).astype(o_ref.dtype)
        elif kind == "conv":
            st = stages[c % 2].at[c // 2]
            if rh == 0:
                first = (i % (SEQ // PROJ_ROWS)) == 0
                last = st[PROJ_ROWS:PROJ_ROWS + top, :]
                st[0:top, :] = jnp.where(first, 0.0, last)
            lo = top + rh * half_rows
            st[lo:lo + half_rows, :] = sub_dot(c, rh)
            rd = stages[c % 2].at[slot0 + c // 2]
            wcols = slice(c * SUB_COLS, (c + 1) * SUB_COLS)
            w = cw_ref[:, wcols]
            y = rd[lo:lo + half_rows, :] * w[SSD_CONV - 1:SSD_CONV, :] + cb_ref[:, wcols]
            for s in range(1, SSD_CONV):
                y = y + rd[lo - s:lo - s + half_rows, :] * w[SSD_CONV - 1 - s:SSD_CONV - s, :]
            o_ref[rows, cols] = _silu(y).astype(o_ref.dtype)
        else:
            acc = sub_dot(c, rh)
            cos = cos_ref[rows, :]
            sin = sin_ref[rows, :]
            scale = Q_SCALE * LOG2_E if kind == "rope_q" else 1.0
            lane = lax.broadcasted_iota(jnp.int32, (half_rows, LANES), 1)
            for h in range(SUB_COLS // HEAD_DIM):
                t = acc[:, h * HEAD_DIM:(h + 1) * HEAD_DIM]
                partner = jnp.where(lane < ROPE_HALF,
                                    pltpu.roll(t, LANES - ROPE_HALF, 1),
                                    pltpu.roll(t, ROPE_HALF, 1))
                emit((t * cos + partner * sin) * scale, o_ref, lc * SUB_COLS + h * HEAD_DIM, c, h, dil, rh)

    for lo_step, hi_step, kind_a, kind_b in plan:
        @pl.when((n >= lo_step) & (n < hi_step))
        def _(kind_a=kind_a, kind_b=kind_b):
            for rh in range(row_split):
                for c in range(SUBS_PER_HALF):
                    epilogue(kind_a, c, rh, oa_ref)
                for c in range(SUBS_PER_HALF, 2 * SUBS_PER_HALF):
                    epilogue(kind_b, c, rh, ob_ref)


_W_Q, _W_K, _W_V = 0, 6, 12
_W_ZATTN, _W_ZSSD, _W_XBC = 18, 20, 28
_MAIN = dict(
    wa=[_W_Q, _W_Q + 1, _W_K, _W_K + 1] + [_W_ZSSD + 2 * k for k in range(4)]
    + [_W_XBC + 2 * k for k in range(6)],
    wb=[_W_V, _W_V + 1, _W_ZATTN, _W_ZATTN + 1] + [_W_ZSSD + 2 * k + 1 for k in range(4)]
    + [_W_XBC + 2 * k + 1 for k in range(6)],
    plan=((0, 2, "rope_q", "plain"), (2, 4, "rope_k", "silu"), (4, 8, "silu", "silu"),
          (8, 14, "conv", "conv")),
    conv_first=8, split_out=False, row_split=1, dtype=BF16)
_DILATED = dict(
    wa=[_W_Q + 2, _W_Q + 3, _W_Q + 4, _W_Q + 5, _W_K + 2, _W_K + 3],
    wb=[_W_V + 2, _W_V + 3, _W_V + 4, _W_V + 5, _W_K + 4, _W_K + 5],
    plan=((0, 2, ("rope_q", 4), ("plain", 4)), (2, 4, ("rope_q", 16), ("plain", 16)),
          (4, 6, ("rope_k", 4), ("rope_k", 16))),
    conv_first=0, split_out=True, row_split=2, dtype=BF16)


def _inproj(xn, w_t, cos, sin, conv_w, conv_b, cfg, name):
    steps = len(cfg["wa"])
    t_spec = pl.BlockSpec((PROJ_ROWS, LANES), lambda n, i: (i, 0))
    conv_col = lambda n, i: (0, jnp.clip(n - cfg["conv_first"], 0, SSD_CONV_CH // COL_TILE - 1))
    w_spec = lambda rows: pl.BlockSpec((None, HALF_TILE, D_MODEL), lambda n, i: (0, _lookup(n, rows), 0))
    if cfg["split_out"]:
        out_specs = [pl.BlockSpec((PROJ_ROWS, HALF_TILE), lambda n, i: (i, n))] * 2
        out_shape = [jax.ShapeDtypeStruct((TOKENS, steps * HALF_TILE), cfg["dtype"])] * 2
    else:
        out_specs = pl.BlockSpec((PROJ_ROWS, COL_TILE), lambda n, i: (i, n))
        out_shape = jax.ShapeDtypeStruct((TOKENS, steps * COL_TILE), cfg["dtype"])
    stage = pltpu.VMEM((2, 8 + PROJ_ROWS, SUB_COLS), F32)
    return pl.pallas_call(
        functools.partial(_inproj_kernel, plan=cfg["plan"], split_out=cfg["split_out"],
                          row_split=cfg["row_split"]),
        grid=(steps, TOKENS // PROJ_ROWS),
        in_specs=[pl.BlockSpec((PROJ_ROWS, D_MODEL), lambda n, i: (i, 0)),
                  w_spec(cfg["wa"]), w_spec(cfg["wb"]), t_spec, t_spec,
                  pl.BlockSpec((SSD_CONV, COL_TILE), conv_col),
                  pl.BlockSpec((1, COL_TILE), conv_col)],
        out_specs=out_specs,
        out_shape=out_shape,
        scratch_shapes=[pltpu.VMEM((COL_TILE, D_MODEL), BF16),
                        pltpu.VMEM((2, COL_TILE // LANES, PROJ_ROWS, LANES), F32)]
        + [stage] * 2,
        compiler_params=_params("arbitrary", "arbitrary"),
        name=name,
    )(xn, w_t, w_t, cos, sin, conv_w, conv_b)


def _gate_kernel(x_ref, wa_ref, wb_ref, b_ref, dtb_col_ref, o_ref, dt_ref, dtt_ref, w_s):
    @pl.when(pl.program_id(1) == 0)
    def _():
        w_s[0:COL_TILE - SSD_HEADS, :] = wa_ref[SSD_HEADS:COL_TILE, :].astype(BF16)
        w_s[COL_TILE - SSD_HEADS:COL_TILE, :] = wb_ref[...].astype(BF16)

    @pl.when(pl.program_id(0) == 0)
    def _():
        w = wa_ref[0:SSD_HEADS, :].astype(BF16)
        dtt = jax.nn.softplus(
            lax.dot_general(w, x_ref[...], NT_DIMS, preferred_element_type=F32) + dtb_col_ref[...])
        dtt_ref[...] = dtt
        dt_ref[...] = dtt.T

    for c in range(COL_TILE // SUB_COLS):
        cols = slice(c * SUB_COLS, (c + 1) * SUB_COLS)
        acc = lax.dot_general(x_ref[...], w_s[cols, :], NT_DIMS, preferred_element_type=F32)
        logits = acc + b_ref[:, cols]
        o_ref[:, cols] = (0.5 + 0.5 * jnp.tanh(0.5 * logits)).astype(o_ref.dtype)


def _gate_proj(xn, w_t, gate_bias, dt_bias):
    n_out = 2 * D_MODEL
    first = COL_DT // COL_TILE
    per = COL_TILE // SSD_HEADS
    n_rows = TOKENS // PROJ_ROWS
    dt_row = lambda n, i: jnp.where(n == 0, i, n_rows - 1)
    return pl.pallas_call(
        _gate_kernel,
        grid=(n_out // COL_TILE, n_rows),
        in_specs=[pl.BlockSpec((PROJ_ROWS, D_MODEL), lambda n, i: (i, 0)),
                  pl.BlockSpec((None, COL_TILE, D_MODEL), lambda n, i: (0, first + n, 0)),
                  pl.BlockSpec((None, SSD_HEADS, D_MODEL), lambda n, i: (0, (first + n + 1) * per, 0)),
                  pl.BlockSpec((1, COL_TILE), lambda n, i: (0, n)),
                  pl.BlockSpec((SSD_HEADS, 1), lambda n, i: (0, 0))],
        out_specs=[pl.BlockSpec((PROJ_ROWS, COL_TILE), lambda n, i: (i, n)),
                   pl.BlockSpec((PROJ_ROWS, SSD_HEADS), lambda n, i: (dt_row(n, i), 0)),
                   pl.BlockSpec((SSD_HEADS, PROJ_ROWS), lambda n, i: (0, dt_row(n, i)))],
        out_shape=[jax.ShapeDtypeStruct((TOKENS, n_out), BF16),
                   jax.ShapeDtypeStruct((TOKENS, SSD_HEADS), F32),
                   jax.ShapeDtypeStruct((SSD_HEADS, TOKENS), F32)],
        scratch_shapes=[pltpu.VMEM((COL_TILE, D_MODEL), BF16)],
        compiler_params=_params("arbitrary", "arbitrary"),
        name="gate_proj",
    )(xn, w_t, w_t, gate_bias, dt_bias.reshape(SSD_HEADS, 1))


def _attn_kernel(q0, k0, v0, q1, k1, v1, q2, k2, v2, z_ref, o_ref, m_s, a_s, d_s, *perm):
    blk = ATTN_BLOCK
    qi = lax.broadcasted_iota(jnp.int32, (blk, blk), 0)
    ki = lax.broadcasted_iota(jnp.int32, (blk, blk), 1)
    mask_cur = ki <= qi
    mask_prev = ki >= qi
    nb = SEQ // blk
    qk_dims = (((2,), (2,)), ((0,), (0,)))
    pv_dims = (((2,), (1,)), ((0,), (0,)))

    for src, dst, dil in zip((q1, k1, v1, q2, k2, v2), perm, (DILATIONS[0],) * 3 + (DILATIONS[1],) * 3):
        sub = SEQ // dil
        cnt = PROJ_ROWS // dil
        for r in range(dil):
            for blk_i in range(SEQ // PROJ_ROWS):
                dst[r * sub + blk_i * cnt:r * sub + (blk_i + 1) * cnt, :] = (
                    src[blk_i * PROJ_ROWS + r * cnt:blk_i * PROJ_ROWS + (r + 1) * cnt, :])

    def group(q_ref, k_ref, v_ref, blocks_per_seq):
        q = q_ref[...].reshape(nb, blk, HEAD_DIM)
        k = k_ref[...].reshape(nb, blk, HEAD_DIM)
        v = jnp.concatenate([v_ref[...], jnp.ones((SEQ, LANES), BF16)], axis=1).reshape(nb, blk, -1)
        s_cur = lax.dot_general(q, k, qk_dims, preferred_element_type=F32)
        s_cur = jnp.where(mask_cur[None], s_cur, -jnp.inf)
        m = jnp.max(s_cur, axis=-1, keepdims=True)
        if blocks_per_seq > 1:
            k_prev = jnp.concatenate([k[:1], k[:-1]], axis=0)
            v_prev = jnp.concatenate([v[:1], v[:-1]], axis=0)
            b_idx = lax.broadcasted_iota(jnp.int32, (nb, 1, 1), 0)
            no_prev = jnp.where(b_idx % blocks_per_seq == 0, -jnp.inf, 0.0)
            s_prev = lax.dot_general(q, k_prev, qk_dims, preferred_element_type=F32)
            s_prev = jnp.where(mask_prev[None], s_prev, -jnp.inf) + no_prev
            m = jnp.maximum(m, jnp.max(s_prev, axis=-1, keepdims=True))
        p = jnp.exp2(s_cur - m)
        acc = lax.dot_general(p.astype(BF16), v, pv_dims, preferred_element_type=F32)
        if blocks_per_seq > 1:
            p = jnp.exp2(s_prev - m)
            acc = acc + lax.dot_general(p.astype(BF16), v_prev, pv_dims, preferred_element_type=F32)
        acc = acc.reshape(SEQ, HEAD_DIM + LANES)
        m = jnp.broadcast_to(m, (nb, blk, LANES)).reshape(SEQ, LANES)
        return m, acc[:, HEAD_DIM:], acc[:, :HEAD_DIM]

    for buf, val in zip((m_s, d_s, a_s), group(q0, k0, v0, nb)):
        buf[0] = val
    for g, dil in enumerate(DILATIONS):
        sub = SEQ // dil
        for buf, val in zip((m_s, d_s, a_s), group(*perm[3 * g:3 * g + 3], sub // blk)):
            for r in range(dil):
                buf[g + 1, pl.ds(r, sub, stride=dil), :] = val[r * sub:(r + 1) * sub, :]

    n_groups = len(DILATIONS) + 1
    m_all = [m_s[g] for g in range(n_groups)]
    m_max = functools.reduce(jnp.maximum, m_all)
    w = [jnp.exp2(m - m_max) for m in m_all]
    num = sum(w[g] * a_s[g] for g in range(n_groups))
    den = sum(w[g] * d_s[g] for g in range(n_groups))
    o_ref[...] = (num / den * z_ref[...].astype(F32)).astype(o_ref.dtype)


def _attention(p_main, dil_a, dil_b):
    hb = HEADS_PER_GROUP

    def spec(sec):
        return pl.BlockSpec((SEQ, HEAD_DIM), lambda b, h: (b, sec * hb + h))

    def main_spec(first):
        per = HALF_TILE // HEAD_DIM
        return pl.BlockSpec((SEQ, HEAD_DIM), lambda b, h: (b, first * per + h + per * (h // per)))

    return pl.pallas_call(
        _attn_kernel,
        grid=(BATCH, hb),
        in_specs=[main_spec(0), main_spec(4), main_spec(1), spec(0), spec(2), spec(0), spec(1), spec(2),
                  spec(1), main_spec(5)],
        out_specs=pl.BlockSpec((SEQ, HEAD_DIM), lambda b, h: (b, h)),
        out_shape=jax.ShapeDtypeStruct((TOKENS, GROUP_WIDTH), BF16),
        scratch_shapes=([pltpu.VMEM((len(DILATIONS) + 1, SEQ, LANES), F32)] * 3
                        + [pltpu.VMEM((SEQ, HEAD_DIM), BF16)] * 6),
        compiler_params=_params("arbitrary", "arbitrary"),
        name="dilated_attn",
    )(p_main, p_main, p_main, dil_a, dil_a, dil_b, dil_a, dil_b, dil_b, p_main)


def _ssd_kernel(xs_ref, b_ref, c_ref, z_ref, dt_ref, dtt_ref, arow_ref, acol_ref, dskip_ref, nw_ref,
                o_ref, rs_ref, state, cs3, cst_s, wrow_s, cdec_s):
    L = SSD_CHUNK
    G = SSD_GROUPS
    J = SSD_HEADS_PER_GROUP
    GW = SSD_GROUP_WIDTH
    N = SSD_STATE

    @pl.when(pl.program_id(1) == 0)
    def _():
        state[...] = jnp.zeros_like(state)

    li = lax.broadcasted_iota(jnp.int32, (L, L), 0)
    si = lax.broadcasted_iota(jnp.int32, (L, L), 1)
    tri = li >= si
    a_row = -jnp.exp(arow_ref[...])
    a_col = -jnp.exp(acol_ref[...])
    dtt = dtt_ref[...]
    cs = LOG2_E * jnp.dot(tri.astype(F32), dt_ref[...] * a_row, precision=HIGHEST,
                          preferred_element_type=F32)
    cst = LOG2_E * jnp.dot(dtt * a_col, (si >= li).astype(F32), precision=HIGHEST,
                           preferred_element_type=F32)
    cs_last = cst[:, L - 1:L]
    cst_s[...] = cst - jnp.log2(dtt)
    wrow_s[...] = dtt * jnp.exp2(cs_last - cst)
    cdec_s[...] = jnp.broadcast_to(jnp.exp2(cs_last), (SSD_HEADS, LANES))
    for g in range(G):
        cs3[g] = cs[:, g * J:(g + 1) * J]

    lane = lax.broadcasted_iota(jnp.int32, (L, LANES), 1)
    lo_half = lane < SSD_HEAD_DIM

    def split(v):
        zero = jnp.zeros_like(v)
        return jnp.where(lo_half, v, zero), jnp.where(lo_half, zero, v)

    def group(g):
        cols = slice(g * GW, (g + 1) * GW)
        ncols = slice(g * N, (g + 1) * N)
        heads = slice(g * J, (g + 1) * J)
        x_g = xs_ref[:, cols]
        b_g = b_ref[:, ncols]
        c_g = c_ref[:, ncols]
        cb = lax.dot_general(c_g, b_g, NT_DIMS, preferred_element_type=F32)
        bt = b_g.astype(F32).T
        cf = c_g.astype(F32)
        cs_g = cs3[g]
        cst_g = cst_s[heads, :]
        w_g = wrow_s[heads, :]
        cd_g = cdec_s[heads, :]
        st_prev = state[g]
        y_parts = []
        s_parts = []
        for p in range(J // 2):
            lhs_y = []
            lhs_s = []
            for j in (2 * p, 2 * p + 1):
                col = jnp.broadcast_to(cs_g[:, j:j + 1], (L, LANES))
                decay = jnp.exp2(jnp.where(tri, col - cst_g[j:j + 1, :], -jnp.inf))
                lhs_y.append((cb * decay).astype(BF16))
                lhs_y.append((cf * jnp.exp2(col)).astype(BF16))
                lhs_s.append((bt * w_g[j:j + 1, :]).astype(BF16))
            lanes_p = slice(p * LANES, (p + 1) * LANES)
            x_lo, x_hi = split(x_g[:, lanes_p])
            s_p = st_prev[:, lanes_p]
            s_lo, s_hi = split(s_p.astype(BF16))
            y_parts.append(jnp.dot(jnp.concatenate(lhs_y, axis=1),
                                   jnp.concatenate([x_lo, s_lo, x_hi, s_hi], axis=0),
                                   preferred_element_type=F32))
            cd = jnp.where(lo_half[0:1, :], cd_g[2 * p:2 * p + 1, :], cd_g[2 * p + 1:2 * p + 2, :])
            s_parts.append(s_p * cd + jnp.dot(jnp.concatenate(lhs_s, axis=1),
                                              jnp.concatenate([x_lo, x_hi], axis=0),
                                              preferred_element_type=F32))
        state[g] = jnp.concatenate(s_parts, axis=1)
        y = jnp.concatenate(y_parts, axis=1) + x_g.astype(F32) * dskip_ref[:, cols]
        y = y * z_ref[:, cols].astype(F32)
        o_ref[:, cols] = (y * nw_ref[:, cols]).astype(o_ref.dtype)
        return jnp.sum(y * y, axis=-1, keepdims=True)

    ssq = group(0)
    for g in range(1, G):
        ssq = ssq + group(g)
    rs_ref[...] = lax.rsqrt(ssq / SSD_INNER + NORM_EPS)


def _ssd(p_main, dt, dt_t, a_log, d_skip, norm_w):
    L = SSD_CHUNK
    row = lambda b, c: b * N_CHUNKS + c
    const = lambda b, c: (0, 0)
    return pl.pallas_call(
        _ssd_kernel,
        grid=(BATCH, N_CHUNKS),
        in_specs=[
            pl.BlockSpec((L, SSD_INNER), lambda b, c: (row(b, c), 2)),
            pl.BlockSpec((L, SSD_BC_WIDTH), lambda b, c: (row(b, c), 12)),
            pl.BlockSpec((L, SSD_BC_WIDTH), lambda b, c: (row(b, c), 13)),
            pl.BlockSpec((L, SSD_INNER), lambda b, c: (row(b, c), 1)),
            pl.BlockSpec((L, SSD_HEADS), lambda b, c: (row(b, c), 0)),
            pl.BlockSpec((SSD_HEADS, L), lambda b, c: (0, row(b, c))),
            pl.BlockSpec((1, SSD_HEADS), const),
            pl.BlockSpec((SSD_HEADS, 1), const),
            pl.BlockSpec((1, SSD_INNER), const),
            pl.BlockSpec((1, SSD_INNER), const),
        ],
        out_specs=[pl.BlockSpec((L, SSD_INNER), lambda b, c: (row(b, c), 0)),
                   pl.BlockSpec((L, 1), lambda b, c: (row(b, c), 0))],
        out_shape=[jax.ShapeDtypeStruct((TOKENS, SSD_INNER), BF16),
                   jax.ShapeDtypeStruct((TOKENS, 1), F32)],
        scratch_shapes=[
            pltpu.VMEM((SSD_GROUPS, SSD_STATE, SSD_GROUP_WIDTH), F32),
            pltpu.VMEM((SSD_GROUPS, L, SSD_HEADS_PER_GROUP), F32),
            pltpu.VMEM((SSD_HEADS, L), F32),
            pltpu.VMEM((SSD_HEADS, L), F32),
            pltpu.VMEM((SSD_HEADS, LANES), F32),
        ],
        compiler_params=_params("arbitrary", "arbitrary"),
        name="ssd",
    )(p_main, p_main, p_main, p_main, dt, dt_t,
      a_log.reshape(1, SSD_HEADS), a_log.reshape(SSD_HEADS, 1),
      jnp.repeat(d_skip, SSD_HEAD_DIM).reshape(1, SSD_INNER), norm_w.reshape(1, SSD_INNER))


def _branch_kernel(a_ref, s_ref, rs_ref, wa_ref, ws_ref, g0_ref, g1_ref, o_ref, wab, wsb):
    @pl.when(pl.program_id(1) == 0)
    def _():
        wab[...] = wa_ref[...].astype(BF16)
        wsb[...] = ws_ref[...].astype(BF16)

    for c in range(o_ref.shape[1] // SUB_COLS):
        cols = slice(c * SUB_COLS, (c + 1) * SUB_COLS)
        ya = jnp.dot(a_ref[...], wab[:, cols], preferred_element_type=F32)
        ys = jnp.dot(s_ref[...], wsb[:, cols], preferred_element_type=F32) * rs_ref[...]
        merged = g0_ref[:, cols].astype(F32) * ya + g1_ref[:, cols].astype(F32) * ys
        o_ref[:, cols] = merged.astype(o_ref.dtype)


def _branches(attn, ssd, ssd_rs, w_attn, w_ssd, gates):
    tn = 512
    nj = D_MODEL // tn
    return pl.pallas_call(
        _branch_kernel,
        grid=(nj, TOKENS // PROJ_ROWS),
        in_specs=[pl.BlockSpec((PROJ_ROWS, GROUP_WIDTH), lambda n, i: (i, 0)),
                  pl.BlockSpec((PROJ_ROWS, SSD_INNER), lambda n, i: (i, 0)),
                  pl.BlockSpec((PROJ_ROWS, 1), lambda n, i: (i, 0)),
                  pl.BlockSpec((None, GROUP_WIDTH, tn), lambda n, i: (0, 0, n)),
                  pl.BlockSpec((None, SSD_INNER, tn), lambda n, i: (0, 0, n)),
                  pl.BlockSpec((PROJ_ROWS, tn), lambda n, i: (i, n)),
                  pl.BlockSpec((PROJ_ROWS, tn), lambda n, i: (i, nj + n))],
        out_specs=pl.BlockSpec((PROJ_ROWS, tn), lambda n, i: (i, n)),
        out_shape=jax.ShapeDtypeStruct((TOKENS, D_MODEL), BF16),
        scratch_shapes=[pltpu.VMEM((GROUP_WIDTH, tn), BF16), pltpu.VMEM((SSD_INNER, tn), BF16)],
        compiler_params=_params("arbitrary", "arbitrary"),
        name="branches",
    )(attn, ssd, ssd_rs, w_attn, w_ssd, gates, gates)


def _out_kernel(m_ref, w_ref, x_ref, fw_ref, o_ref, wb_ref):
    @pl.when(pl.program_id(0) == 0)
    def _():
        wb_ref[...] = w_ref[...].astype(BF16)

    sub = 2 * SUB_COLS
    ssq = jnp.zeros((o_ref.shape[0], 1), F32)
    for c in range(D_MODEL // sub):
        cols = slice(c * sub, (c + 1) * sub)
        h = x_ref[:, cols] + jnp.dot(m_ref[...], wb_ref[:, cols], preferred_element_type=F32)
        ssq = ssq + jnp.sum(h * h, axis=-1, keepdims=True)
        o_ref[:, cols] = h
    o_ref[...] = o_ref[...] * lax.rsqrt(ssq / D_MODEL + NORM_EPS) * fw_ref[...]


def _out_proj(merged, w_out, x2, final_w):
    tm = ROW_TILE
    return pl.pallas_call(
        _out_kernel,
        grid=(TOKENS // tm,),
        in_specs=[pl.BlockSpec((tm, D_MODEL), lambda i: (i, 0)),
                  pl.BlockSpec((None, D_MODEL, D_MODEL), lambda i: (0, 0, 0), pipeline_mode=pl.Buffered(1)),
                  pl.BlockSpec((tm, D_MODEL), lambda i: (i, 0)),
                  pl.BlockSpec((1, D_MODEL), lambda i: (0, 0))],
        out_specs=pl.BlockSpec((tm, D_MODEL), lambda i: (i, 0)),
        out_shape=jax.ShapeDtypeStruct((TOKENS, D_MODEL), F32),
        scratch_shapes=[pltpu.VMEM((D_MODEL, D_MODEL), BF16)],
        compiler_params=_params("arbitrary"),
        name="out_proj",
    )(merged, w_out, x2, final_w.reshape(1, D_MODEL))


def kernel(x, positions, norm_w, w_in, conv_w, conv_b, dt_bias, a_log, d_skip, ssd_norm_w,
           w_attn_br, w_ssd_br, gate_b, w_out, final_norm_w):
    assert x.shape == (BATCH, SEQ, D_MODEL) and w_in.shape[0] == 1
    x2 = x.reshape(TOKENS, D_MODEL)
    xn, cos, sin = _prep(x2, norm_w[0], positions)
    w_t = jnp.swapaxes(w_in, 1, 2)
    cw, cb = conv_w[0], conv_b.reshape(1, SSD_CONV_CH)
    p_main = _inproj(xn, w_t, cos, sin, cw, cb, _MAIN, "inproj_main")
    dil_a, dil_b = _inproj(xn, w_t, cos, sin, cw, cb, _DILATED, "inproj_dilated")
    gates, dt, dt_t = _gate_proj(xn, w_t, gate_b[0].reshape(1, 2 * D_MODEL), dt_bias[0])
    attn = _attention(p_main, dil_a, dil_b)
    ssd, ssd_rs = _ssd(p_main, dt, dt_t, a_log[0], d_skip[0], ssd_norm_w[0])
    merged = _branches(attn, ssd, ssd_rs, w_attn_br, w_ssd_br, gates)
    out = _out_proj(merged, w_out, x2, final_norm_w)
    return out.reshape(BATCH, SEQ, D_MODEL)
```

```python
import functools

import jax
import jax.numpy as jnp
from jax import lax
from jax.experimental import pallas as pl
from jax.experimental.pallas import tpu as pltpu

F32 = jnp.float32
BF16 = jnp.bfloat16
HIGHEST = lax.Precision.HIGHEST

D_MODEL = 2048
BATCH = 4
SEQ = 2048
TOKENS = BATCH * SEQ
NORM_EPS = 1e-5

HEAD_DIM = 128
HEADS_PER_GROUP = 8
GROUP_WIDTH = HEADS_PER_GROUP * HEAD_DIM
ATTN_BLOCK = 128
ATTN_CHUNK_BLOCKS = 8
DILATIONS = (4, 16)
ROPE_DIMS = HEAD_DIM // 4
ROPE_HALF = ROPE_DIMS // 2
ROPE_THETA = 500000.0
Q_SCALE = HEAD_DIM ** -0.5
LOG2_E = 1.4426950408889634

SSD_INNER = 2 * D_MODEL
SSD_HEAD_DIM = 64
SSD_HEADS = SSD_INNER // SSD_HEAD_DIM
SSD_GROUPS = 8
SSD_HEADS_PER_GROUP = SSD_HEADS // SSD_GROUPS
SSD_STATE = 128
SSD_CONV = 4
SSD_CHUNK = 128
SSD_GROUP_WIDTH = SSD_HEADS_PER_GROUP * SSD_HEAD_DIM
SSD_BC_WIDTH = SSD_GROUPS * SSD_STATE
SSD_CONV_CH = SSD_INNER + 2 * SSD_BC_WIDTH
N_CHUNKS = SEQ // SSD_CHUNK

COL_DT = 3 * 3 * GROUP_WIDTH + GROUP_WIDTH + SSD_INNER + SSD_CONV_CH
COL_GATES = COL_DT + SSD_HEADS

LANES = 128
ROW_TILE = 512
PROJ_ROWS = 1024
COL_TILE = 1024
HALF_TILE = COL_TILE // 2
SUB_COLS = 256
SUBS_PER_HALF = HALF_TILE // SUB_COLS
DEINT_STRIDE = 4
VMEM_LIMIT = 56 * 1024 * 1024


NT_DIMS = (((1,), (1,)), ((), ()))


def _params(*sem):
    return pltpu.CompilerParams(dimension_semantics=sem, vmem_limit_bytes=VMEM_LIMIT)


def _prep_kernel(x_ref, w_ref, pos_ref, invf_ref, xn_ref, cos_ref, sin_ref):
    x = x_ref[...]
    ms = jnp.mean(x * x, axis=-1, keepdims=True)
    xn_ref[...] = (x * lax.rsqrt(ms + NORM_EPS) * w_ref[...]).astype(xn_ref.dtype)
    ang = pos_ref[...].astype(F32) * invf_ref[...]
    lane = lax.broadcasted_iota(jnp.int32, ang.shape, 1)
    sign = jnp.where(lane < ROPE_HALF, -1.0, jnp.where(lane < ROPE_DIMS, 1.0, 0.0))
    cos_ref[...] = jnp.cos(ang)
    sin_ref[...] = jnp.sin(ang) * sign


def _prep(x2, w, positions):
    inv_freq = ROPE_THETA ** (-jnp.arange(ROPE_HALF, dtype=F32) / ROPE_HALF)
    invf = jnp.zeros((1, LANES), F32).at[0, :ROPE_DIMS].set(jnp.tile(inv_freq, 2))
    row_spec = lambda width: pl.BlockSpec((ROW_TILE, width), lambda i: (i, 0))
    const_spec = lambda width: pl.BlockSpec((1, width), lambda i: (0, 0))
    return pl.pallas_call(
        _prep_kernel,
        grid=(TOKENS // ROW_TILE,),
        in_specs=[row_spec(D_MODEL), const_spec(D_MODEL), row_spec(1), const_spec(LANES)],
        out_specs=[row_spec(D_MODEL), row_spec(LANES), row_spec(LANES)],
        out_shape=[jax.ShapeDtypeStruct((TOKENS, D_MODEL), BF16),
                   jax.ShapeDtypeStruct((TOKENS, LANES), F32),
                   jax.ShapeDtypeStruct((TOKENS, LANES), F32)],
        compiler_params=_params("arbitrary"),
        name="prep",
    )(x2, w.reshape(1, D_MODEL), positions.reshape(TOKENS, 1), invf)


def _silu(v):
    return 0.5 * v * (1.0 + jnp.tanh(0.5 * v))


def _lookup(n, values):
    out = values[-1]
    for k in range(len(values) - 2, -1, -1):
        out = jnp.where(n == k, values[k], out)
    return out


def _inproj_kernel(x_ref, wa_ref, wb_ref, cos_ref, sin_ref, cw_ref, cb_ref, *refs, plan, split_out,
                   row_split):
    if split_out:
        oa_ref, ob_ref, w_s, deint_ref, *stages = refs
    else:
        o_ref, w_s, deint_ref, *stages = refs
        oa_ref = o_ref.at[:, 0:HALF_TILE]
        ob_ref = o_ref.at[:, HALF_TILE:COL_TILE]
    n = pl.program_id(0)
    i = pl.program_id(1)

    @pl.when(i == 0)
    def _():
        w_s[0:HALF_TILE, :] = wa_ref[...].astype(BF16)
        w_s[HALF_TILE:COL_TILE, :] = wb_ref[...].astype(BF16)

    half_rows = PROJ_ROWS // row_split

    def sub_dot(c, rh):
        rows = slice(rh * half_rows, (rh + 1) * half_rows)
        return lax.dot_general(x_ref[rows, :], w_s[c * SUB_COLS:(c + 1) * SUB_COLS, :], NT_DIMS,
                               preferred_element_type=F32)

    top = stages[0].shape[1] - PROJ_ROWS
    slot0 = lax.shift_right_logical(i, 30)

    def emit(val, o_ref, col, c, h, dil, rh):
        rows = slice(rh * half_rows, (rh + 1) * half_rows)
        if dil == 1:
            o_ref[rows, col:col + LANES] = val.astype(o_ref.dtype)
            return
        slot = c * (SUB_COLS // LANES) + h
        slab = deint_ref.at[0, slot]
        slab[rows, :] = val
        if rh < row_split - 1:
            return
        groups = 1
        if dil != DEINT_STRIDE:
            assert dil == DEINT_STRIDE ** 2
            by_low = deint_ref.at[1, slot]
            cnt = PROJ_ROWS // DEINT_STRIDE
            for low in range(DEINT_STRIDE):
                by_low[low * cnt:(low + 1) * cnt, :] = slab[pl.ds(low, cnt, stride=DEINT_STRIDE), :]
            slab, groups = by_low, DEINT_STRIDE
        per_group = PROJ_ROWS // groups
        cnt = PROJ_ROWS // dil
        for low in range(groups):
            for high in range(DEINT_STRIDE):
                r = high * groups + low
                o_ref[r * cnt:(r + 1) * cnt, col:col + LANES] = (
                    slab[pl.ds(low * per_group + high, cnt, stride=DEINT_STRIDE), :].astype(o_ref.dtype))

    def epilogue(kind, c, rh, o_ref):
        kind, dil = kind if isinstance(kind, tuple) else (kind, 1)
        lc = c % SUBS_PER_HALF
        cols = slice(lc * SUB_COLS, (lc + 1) * SUB_COLS)
        rows = slice(rh * half_rows, (rh + 1) * half_rows)
        if kind == "plain" and dil > 1:
            acc = sub_dot(c, rh)
            for h in range(SUB_COLS // LANES):
                emit(acc[:, h * LANES:(h + 1) * LANES], o_ref, lc * SUB_COLS + h * LANES, c, h, dil, rh)
        elif kind == "plain":
            o_ref[rows, cols] = sub_dot(c, rh).astype(o_ref.dtype)
        elif kind == "silu":
            o_ref[rows, cols] = _silu(sub_dot(c, rh)).astype(o_ref.dtype)
        elif kind == "conv":
            st = stages[c % 2].at[c // 2]
            if rh == 0:
                first = (i % (SEQ // PROJ_ROWS)) == 0
                last = st[PROJ_ROWS:PROJ_ROWS + top, :]
                st[0:top, :] = jnp.where(first, 0.0, last)
            lo = top + rh * half_rows
            st[lo:lo + half_rows, :] = sub_dot(c, rh)
            rd = stages[c % 2].at[slot0 + c // 2]
            wcols = slice(c * SUB_COLS, (c + 1) * SUB_COLS)
            w = cw_ref[:, wcols]
            y = rd[lo:lo + half_rows, :] * w[SSD_CONV - 1:SSD_CONV, :] + cb_ref[:, wcols]
            for s in range(1, SSD_CONV):
                y = y + rd[lo - s:lo - s + half_rows, :] * w[SSD_CONV - 1 - s:SSD_CONV - s, :]
            o_ref[rows, cols] = _silu(y).astype(o_ref.dtype)
        else:
            acc = sub_dot(c, rh)
            cos = cos_ref[rows, :]
            sin = sin_ref[rows, :]
            scale = Q_SCALE * LOG2_E if kind == "rope_q" else 1.0
            lane = lax.broadcasted_iota(jnp.int32, (half_rows, LANES), 1)
            for h in range(SUB_COLS // HEAD_DIM):
                t = acc[:, h * HEAD_DIM:(h + 1) * HEAD_DIM]
                partner = jnp.where(lane < ROPE_HALF,
                                    pltpu.roll(t, LANES - ROPE_HALF, 1),
                                    pltpu.roll(t, ROPE_HALF, 1))
                emit((t * cos + partner * sin) * scale, o_ref, lc * SUB_COLS + h * HEAD_DIM, c, h, dil, rh)

    for lo_step, hi_step, kind_a, kind_b in plan:
        @pl.when((n >= lo_step) & (n < hi_step))
        def _(kind_a=kind_a, kind_b=kind_b):
            for rh in range(row_split):
                for c in range(SUBS_PER_HALF):
                    epilogue(kind_a, c, rh, oa_ref)
                for c in range(SUBS_PER_HALF, 2 * SUBS_PER_HALF):
                    epilogue(kind_b, c, rh, ob_ref)


_W_Q, _W_K, _W_V = 0, 6, 12
_W_ZATTN, _W_ZSSD, _W_XBC = 18, 20, 28
_MAIN = dict(
    wa=[_W_Q, _W_Q + 1, _W_K, _W_K + 1] + [_W_ZSSD + 2 * k for k in range(4)]
    + [_W_XBC + 2 * k for k in range(6)],
    wb=[_W_V, _W_V + 1, _W_ZATTN, _W_ZATTN + 1] + [_W_ZSSD + 2 * k + 1 for k in range(4)]
    + [_W_XBC + 2 * k + 1 for k in range(6)],
    plan=((0, 2, "rope_q", "plain"), (2, 4, "rope_k", "silu"), (4, 8, "silu", "silu"),
          (8, 14, "conv", "conv")),
    conv_first=8, split_out=False, row_split=1, dtype=BF16)
_DILATED = dict(
    wa=[_W_Q + 2, _W_Q + 3, _W_Q + 4, _W_Q + 5, _W_K + 2, _W_K + 3],
    wb=[_W_V + 2, _W_V + 3, _W_V + 4, _W_V + 5, _W_K + 4, _W_K + 5],
    plan=((0, 2, ("rope_q", 4), ("plain", 4)), (2, 4, ("rope_q", 16), ("plain", 16)),
          (4, 6, ("rope_k", 4), ("rope_k", 16))),
    conv_first=0, split_out=True, row_split=2, dtype=BF16)


def _inproj(xn, w_t, cos, sin, conv_w, conv_b, cfg, name):
    steps = len(cfg["wa"])
    t_spec = pl.BlockSpec((PROJ_ROWS, LANES), lambda n, i: (i, 0))
    conv_col = lambda n, i: (0, jnp.clip(n - cfg["conv_first"], 0, SSD_CONV_CH // COL_TILE - 1))
    w_spec = lambda rows: pl.BlockSpec((None, HALF_TILE, D_MODEL), lambda n, i: (0, _lookup(n, rows), 0))
    if cfg["split_out"]:
        out_specs = [pl.BlockSpec((PROJ_ROWS, HALF_TILE), lambda n, i: (i, n))] * 2
        out_shape = [jax.ShapeDtypeStruct((TOKENS, steps * HALF_TILE), cfg["dtype"])] * 2
    else:
        out_specs = pl.BlockSpec((PROJ_ROWS, COL_TILE), lambda n, i: (i, n))
        out_shape = jax.ShapeDtypeStruct((TOKENS, steps * COL_TILE), cfg["dtype"])
    stage = pltpu.VMEM((2, 8 + PROJ_ROWS, SUB_COLS), F32)
    return pl.pallas_call(
        functools.partial(_inproj_kernel, plan=cfg["plan"], split_out=cfg["split_out"],
                          row_split=cfg["row_split"]),
        grid=(steps, TOKENS // PROJ_ROWS),
        in_specs=[pl.BlockSpec((PROJ_ROWS, D_MODEL), lambda n, i: (i, 0)),
                  w_spec(cfg["wa"]), w_spec(cfg["wb"]), t_spec, t_spec,
                  pl.BlockSpec((SSD_CONV, COL_TILE), conv_col),
                  pl.BlockSpec((1, COL_TILE), conv_col)],
        out_specs=out_specs,
        out_shape=out_shape,
        scratch_shapes=[pltpu.VMEM((COL_TILE, D_MODEL), BF16),
                        pltpu.VMEM((2, COL_TILE // LANES, PROJ_ROWS, LANES), F32)]
        + [stage] * 2,
        compiler_params=_params("arbitrary", "arbitrary"),
        name=name,
    )(xn, w_t, w_t, cos, sin, conv_w, conv_b)


def _gate_kernel(x_ref, wa_ref, wb_ref, b_ref, dtb_col_ref, o_ref, dt_ref, dtt_ref, w_s):
    @pl.when(pl.program_id(1) == 0)
    def _():
        w_s[0:COL_TILE - SSD_HEADS, :] = wa_ref[SSD_HEADS:COL_TILE, :].astype(BF16)
        w_s[COL_TILE - SSD_HEADS:COL_TILE, :] = wb_ref[...].astype(BF16)

    @pl.when(pl.program_id(0) == 0)
    def _():
        w = wa_ref[0:SSD_HEADS, :].astype(BF16)
        dtt = jax.nn.softplus(
            lax.dot_general(w, x_ref[...], NT_DIMS, preferred_element_type=F32) + dtb_col_ref[...])
        dtt_ref[...] = dtt
        dt_ref[...] = dtt.T

    for c in range(COL_TILE // SUB_COLS):
        cols = slice(c * SUB_COLS, (c + 1) * SUB_COLS)
        acc = lax.dot_general(x_ref[...], w_s[cols, :], NT_DIMS, preferred_element_type=F32)
        logits = acc + b_ref[:, cols]
        o_ref[:, cols] = (0.5 + 0.5 * jnp.tanh(0.5 * logits)).astype(o_ref.dtype)


def _gate_proj(xn, w_t, gate_bias, dt_bias):
    n_out = 2 * D_MODEL
    first = COL_DT // COL_TILE
    per = COL_TILE // SSD_HEADS
    n_rows = TOKENS // PROJ_ROWS
    dt_row = lambda n, i: jnp.where(n == 0, i, n_rows - 1)
    return pl.pallas_call(
        _gate_kernel,
        grid=(n_out // COL_TILE, n_rows),
        in_specs=[pl.BlockSpec((PROJ_ROWS, D_MODEL), lambda n, i: (i, 0)),
                  pl.BlockSpec((None, COL_TILE, D_MODEL), lambda n, i: (0, first + n, 0)),
                  pl.BlockSpec((None, SSD_HEADS, D_MODEL), lambda n, i: (0, (first + n + 1) * per, 0)),
                  pl.BlockSpec((1, COL_TILE), lambda n, i: (0, n)),
                  pl.BlockSpec((SSD_HEADS, 1), lambda n, i: (0, 0))],
        out_specs=[pl.BlockSpec((PROJ_ROWS, COL_TILE), lambda n, i: (i, n)),
                   pl.BlockSpec((PROJ_ROWS, SSD_HEADS), lambda n, i: (dt_row(n, i), 0)),
                   pl.BlockSpec((SSD_HEADS, PROJ_ROWS), lambda n, i: (0, dt_row(n, i)))],
        out_shape=[jax.ShapeDtypeStruct((TOKENS, n_out), BF16),
                   jax.ShapeDtypeStruct((TOKENS, SSD_HEADS), F32),
                   jax.ShapeDtypeStruct((SSD_HEADS, TOKENS), F32)],
        scratch_shapes=[pltpu.VMEM((COL_TILE, D_MODEL), BF16)],
        compiler_params=_params("arbitrary", "arbitrary"),
        name="gate_proj",
    )(xn, w_t, w_t, gate_bias, dt_bias.reshape(SSD_HEADS, 1))


def _attn_kernel(q0, k0, v0, q1, k1, v1, q2, k2, v2, z_ref, o_ref, m_s, a_s, d_s, *perm):
    blk = ATTN_BLOCK
    qi = lax.broadcasted_iota(jnp.int32, (blk, blk), 0)
    ki = lax.broadcasted_iota(jnp.int32, (blk, blk), 1)
    mask_cur = ki <= qi
    mask_prev = ki >= qi
    nb = SEQ // blk
    qk_dims = (((2,), (2,)), ((0,), (0,)))
    pv_dims = (((2,), (1,)), ((0,), (0,)))

    for src, dst, dil in zip((q1, k1, v1, q2, k2, v2), perm, (DILATIONS[0],) * 3 + (DILATIONS[1],) * 3):
        sub = SEQ // dil
        cnt = PROJ_ROWS // dil
        for r in range(dil):
            for blk_i in range(SEQ // PROJ_ROWS):
                dst[r * sub + blk_i * cnt:r * sub + (blk_i + 1) * cnt, :] = (
                    src[blk_i * PROJ_ROWS + r * cnt:blk_i * PROJ_ROWS + (r + 1) * cnt, :])

    nbc = ATTN_CHUNK_BLOCKS
    chunk_rows = nbc * blk

    def with_ones(v2d):
        return jnp.concatenate([v2d, jnp.ones((v2d.shape[0], LANES), BF16)], axis=1)

    def group(q_ref, k_ref, v_ref, blocks_per_seq, t):
        start = pl.multiple_of(t * chunk_rows, chunk_rows)
        rows = pl.ds(start, chunk_rows)
        q = q_ref[rows, :].reshape(nbc, blk, HEAD_DIM)
        k = k_ref[rows, :].reshape(nbc, blk, HEAD_DIM)
        v = with_ones(v_ref[rows, :]).reshape(nbc, blk, -1)
        s_cur = lax.dot_general(q, k, qk_dims, preferred_element_type=F32)
        s_cur = jnp.where(mask_cur[None], s_cur, -jnp.inf)
        m = jnp.max(s_cur, axis=-1, keepdims=True)
        if blocks_per_seq > 1:
            before = pl.ds(pl.multiple_of(jnp.maximum(start - blk, 0), blk), blk)
            k_prev = jnp.concatenate([k_ref[before, :].reshape(1, blk, HEAD_DIM), k[:-1]], axis=0)
            v_prev = jnp.concatenate([with_ones(v_ref[before, :]).reshape(1, blk, -1), v[:-1]], axis=0)
            b_idx = lax.broadcasted_iota(jnp.int32, (nbc, 1, 1), 0) + t * nbc
            no_prev = jnp.where(b_idx % blocks_per_seq == 0, -jnp.inf, 0.0)
            s_prev = lax.dot_general(q, k_prev, qk_dims, preferred_element_type=F32)
            s_prev = jnp.where(mask_prev[None], s_prev, -jnp.inf) + no_prev
            m = jnp.maximum(m, jnp.max(s_prev, axis=-1, keepdims=True))
        p = jnp.exp2(s_cur - m)
        acc = lax.dot_general(p.astype(BF16), v, pv_dims, preferred_element_type=F32)
        if blocks_per_seq > 1:
            p = jnp.exp2(s_prev - m)
            acc = acc + lax.dot_general(p.astype(BF16), v_prev, pv_dims, preferred_element_type=F32)
        acc = acc.reshape(chunk_rows, HEAD_DIM + LANES)
        m = jnp.broadcast_to(m, (nbc, blk, LANES)).reshape(chunk_rows, LANES)
        return m, acc[:, HEAD_DIM:], acc[:, :HEAD_DIM]

    def chunk0(t, carry):
        rows = pl.ds(pl.multiple_of(t * chunk_rows, chunk_rows), chunk_rows)
        for buf, val in zip((m_s, d_s, a_s), group(q0, k0, v0, nb, t)):
            buf[0, rows, :] = val
        return carry

    lax.fori_loop(0, nb // nbc, chunk0, 0)
    for g, dil in enumerate(DILATIONS):
        sub = SEQ // dil
        per_chunk = chunk_rows // sub

        def chunk(t, carry, g=g, dil=dil, sub=sub, per_chunk=per_chunk):
            for buf, val in zip((m_s, d_s, a_s), group(*perm[3 * g:3 * g + 3], sub // blk, t)):
                for rr in range(per_chunk):
                    r = t * per_chunk + rr
                    buf[g + 1, pl.ds(r, sub, stride=dil), :] = val[rr * sub:(rr + 1) * sub, :]
            return carry

        lax.fori_loop(0, nb // nbc, chunk, 0)

    n_groups = len(DILATIONS) + 1
    m_all = [m_s[g] for g in range(n_groups)]
    m_max = functools.reduce(jnp.maximum, m_all)
    w = [jnp.exp2(m - m_max) for m in m_all]
    num = sum(w[g] * a_s[g] for g in range(n_groups))
    den = sum(w[g] * d_s[g] for g in range(n_groups))
    o_ref[...] = (num / den * z_ref[...].astype(F32)).astype(o_ref.dtype)


def _attention(p_main, dil_a, dil_b):
    hb = HEADS_PER_GROUP

    def spec(sec):
        return pl.BlockSpec((SEQ, HEAD_DIM), lambda b, h: (b, sec * hb + h))

    def main_spec(first):
        per = HALF_TILE // HEAD_DIM
        return pl.BlockSpec((SEQ, HEAD_DIM), lambda b, h: (b, first * per + h + per * (h // per)))

    return pl.pallas_call(
        _attn_kernel,
        grid=(BATCH, hb),
        in_specs=[main_spec(0), main_spec(4), main_spec(1), spec(0), spec(2), spec(0), spec(1), spec(2),
                  spec(1), main_spec(5)],
        out_specs=pl.BlockSpec((SEQ, HEAD_DIM), lambda b, h: (b, h)),
        out_shape=jax.ShapeDtypeStruct((TOKENS, GROUP_WIDTH), BF16),
        scratch_shapes=([pltpu.VMEM((len(DILATIONS) + 1, SEQ, LANES), F32)] * 3
                        + [pltpu.VMEM((SEQ, HEAD_DIM), BF16)] * 6),
        compiler_params=_params("arbitrary", "arbitrary"),
        name="dilated_attn",
    )(p_main, p_main, p_main, dil_a, dil_a, dil_b, dil_a, dil_b, dil_b, p_main)


def _ssd_kernel(xs_ref, b_ref, c_ref, z_ref, dt_ref, dtt_ref, arow_ref, acol_ref, dskip_ref, nw_ref,
                o_ref, rs_ref, state, cs3, cst_s, wrow_s, cdec_s):
    L = SSD_CHUNK
    G = SSD_GROUPS
    J = SSD_HEADS_PER_GROUP
    GW = SSD_GROUP_WIDTH
    N = SSD_STATE

    @pl.when(pl.program_id(1) == 0)
    def _():
        state[...] = jnp.zeros_like(state)

    li = lax.broadcasted_iota(jnp.int32, (L, L), 0)
    si = lax.broadcasted_iota(jnp.int32, (L, L), 1)
    tri = li >= si
    a_row = -jnp.exp(arow_ref[...])
    a_col = -jnp.exp(acol_ref[...])
    dtt = dtt_ref[...]
    cs = LOG2_E * jnp.dot(tri.astype(F32), dt_ref[...] * a_row, precision=HIGHEST,
                          preferred_element_type=F32)
    cst = LOG2_E * jnp.dot(dtt * a_col, (si >= li).astype(F32), precision=HIGHEST,
                           preferred_element_type=F32)
    cs_last = cst[:, L - 1:L]
    cst_s[...] = cst - jnp.log2(dtt)
    wrow_s[...] = dtt * jnp.exp2(cs_last - cst)
    cdec_s[...] = jnp.broadcast_to(jnp.exp2(cs_last), (SSD_HEADS, LANES))
    for g in range(G):
        cs3[g] = cs[:, g * J:(g + 1) * J]

    lane = lax.broadcasted_iota(jnp.int32, (L, LANES), 1)
    lo_half = lane < SSD_HEAD_DIM

    def split(v):
        zero = jnp.zeros_like(v)
        return jnp.where(lo_half, v, zero), jnp.where(lo_half, zero, v)

    def group(g):
        cols = slice(g * GW, (g + 1) * GW)
        ncols = slice(g * N, (g + 1) * N)
        heads = slice(g * J, (g + 1) * J)
        x_g = xs_ref[:, cols]
        b_g = b_ref[:, ncols]
        c_g = c_ref[:, ncols]
        cb = lax.dot_general(c_g, b_g, NT_DIMS, preferred_element_type=F32)
        bt = b_g.astype(F32).T
        cf = c_g.astype(F32)
        cs_g = cs3[g]
        cst_g = cst_s[heads, :]
        w_g = wrow_s[heads, :]
        cd_g = cdec_s[heads, :]
        st_prev = state[g]
        y_parts = []
        s_parts = []
        for p in range(J // 2):
            lhs_y = []
            lhs_s = []
            for j in (2 * p, 2 * p + 1):
                col = jnp.broadcast_to(cs_g[:, j:j + 1], (L, LANES))
                decay = jnp.exp2(jnp.where(tri, col - cst_g[j:j + 1, :], -jnp.inf))
                lhs_y.append((cb * decay).astype(BF16))
                lhs_y.append((cf * jnp.exp2(col)).astype(BF16))
                lhs_s.append((bt * w_g[j:j + 1, :]).astype(BF16))
            lanes_p = slice(p * LANES, (p + 1) * LANES)
            x_lo, x_hi = split(x_g[:, lanes_p])
            s_p = st_prev[:, lanes_p]
            s_lo, s_hi = split(s_p.astype(BF16))
            y_parts.append(jnp.dot(jnp.concatenate(lhs_y, axis=1),
                                   jnp.concatenate([x_lo, s_lo, x_hi, s_hi], axis=0),
                                   preferred_element_type=F32))
            cd = jnp.where(lo_half[0:1, :], cd_g[2 * p:2 * p + 1, :], cd_g[2 * p + 1:2 * p + 2, :])
            s_parts.append(s_p * cd + jnp.dot(jnp.concatenate(lhs_s, axis=1),
                                              jnp.concatenate([x_lo, x_hi], axis=0),
                                              preferred_element_type=F32))
        state[g] = jnp.concatenate(s_parts, axis=1)
        y = jnp.concatenate(y_parts, axis=1) + x_g.astype(F32) * dskip_ref[:, cols]
        y = y * z_ref[:, cols].astype(F32)
        o_ref[:, cols] = (y * nw_ref[:, cols]).astype(o_ref.dtype)
        return jnp.sum(y * y, axis=-1, keepdims=True)

    ssq = group(0)
    for g in range(1, G):
        ssq = ssq + group(g)
    rs_ref[...] = lax.rsqrt(ssq / SSD_INNER + NORM_EPS)


def _ssd(p_main, dt, dt_t, a_log, d_skip, norm_w):
    L = SSD_CHUNK
    row = lambda b, c: b * N_CHUNKS + c
    const = lambda b, c: (0, 0)
    return pl.pallas_call(
        _ssd_kernel,
        grid=(BATCH, N_CHUNKS),
        in_specs=[
            pl.BlockSpec((L, SSD_INNER), lambda b, c: (row(b, c), 2)),
            pl.BlockSpec((L, SSD_BC_WIDTH), lambda b, c: (row(b, c), 12)),
            pl.BlockSpec((L, SSD_BC_WIDTH), lambda b, c: (row(b, c), 13)),
            pl.BlockSpec((L, SSD_INNER), lambda b, c: (row(b, c), 1)),
            pl.BlockSpec((L, SSD_HEADS), lambda b, c: (row(b, c), 0)),
            pl.BlockSpec((SSD_HEADS, L), lambda b, c: (0, row(b, c))),
            pl.BlockSpec((1, SSD_HEADS), const),
            pl.BlockSpec((SSD_HEADS, 1), const),
            pl.BlockSpec((1, SSD_INNER), const),
            pl.BlockSpec((1, SSD_INNER), const),
        ],
        out_specs=[pl.BlockSpec((L, SSD_INNER), lambda b, c: (row(b, c), 0)),
                   pl.BlockSpec((L, 1), lambda b, c: (row(b, c), 0))],
        out_shape=[jax.ShapeDtypeStruct((TOKENS, SSD_INNER), BF16),
                   jax.ShapeDtypeStruct((TOKENS, 1), F32)],
        scratch_shapes=[
            pltpu.VMEM((SSD_GROUPS, SSD_STATE, SSD_GROUP_WIDTH), F32),
            pltpu.VMEM((SSD_GROUPS, L, SSD_HEADS_PER_GROUP), F32),
            pltpu.VMEM((SSD_HEADS, L), F32),
            pltpu.VMEM((SSD_HEADS, L), F32),
            pltpu.VMEM((SSD_HEADS, LANES), F32),
        ],
        compiler_params=_params("arbitrary", "arbitrary"),
        name="ssd",
    )(p_main, p_main, p_main, p_main, dt, dt_t,
      a_log.reshape(1, SSD_HEADS), a_log.reshape(SSD_HEADS, 1),
      jnp.repeat(d_skip, SSD_HEAD_DIM).reshape(1, SSD_INNER), norm_w.reshape(1, SSD_INNER))


def _branch_kernel(a_ref, s_ref, rs_ref, wa_ref, ws_ref, g0_ref, g1_ref, o_ref, wab, wsb):
    @pl.when(pl.program_id(1) == 0)
    def _():
        wab[...] = wa_ref[...].astype(BF16)
        wsb[...] = ws_ref[...].astype(BF16)

    for c in range(o_ref.shape[1] // SUB_COLS):
        cols = slice(c * SUB_COLS, (c + 1) * SUB_COLS)
        ya = jnp.dot(a_ref[...], wab[:, cols], preferred_element_type=F32)
        ys = jnp.dot(s_ref[...], wsb[:, cols], preferred_element_type=F32) * rs_ref[...]
        merged = g0_ref[:, cols].astype(F32) * ya + g1_ref[:, cols].astype(F32) * ys
        o_ref[:, cols] = merged.astype(o_ref.dtype)


def _branches(attn, ssd, ssd_rs, w_attn, w_ssd, gates):
    tn = 512
    nj = D_MODEL // tn
    return pl.pallas_call(
        _branch_kernel,
        grid=(nj, TOKENS // PROJ_ROWS),
        in_specs=[pl.BlockSpec((PROJ_ROWS, GROUP_WIDTH), lambda n, i: (i, 0)),
                  pl.BlockSpec((PROJ_ROWS, SSD_INNER), lambda n, i: (i, 0)),
                  pl.BlockSpec((PROJ_ROWS, 1), lambda n, i: (i, 0)),
                  pl.BlockSpec((None, GROUP_WIDTH, tn), lambda n, i: (0, 0, n)),
                  pl.BlockSpec((None, SSD_INNER, tn), lambda n, i: (0, 0, n)),
                  pl.BlockSpec((PROJ_ROWS, tn), lambda n, i: (i, n)),
                  pl.BlockSpec((PROJ_ROWS, tn), lambda n, i: (i, nj + n))],
        out_specs=pl.BlockSpec((PROJ_ROWS, tn), lambda n, i: (i, n)),
        out_shape=jax.ShapeDtypeStruct((TOKENS, D_MODEL), BF16),
        scratch_shapes=[pltpu.VMEM((GROUP_WIDTH, tn), BF16), pltpu.VMEM((SSD_INNER, tn), BF16)],
        compiler_params=_params("arbitrary", "arbitrary"),
        name="branches",
    )(attn, ssd, ssd_rs, w_attn, w_ssd, gates, gates)


def _out_kernel(m_ref, w_ref, x_ref, fw_ref, o_ref, wb_ref):
    @pl.when(pl.program_id(0) == 0)
    def _():
        wb_ref[...] = w_ref[...].astype(BF16)

    sub = 2 * SUB_COLS
    ssq = jnp.zeros((o_ref.shape[0], 1), F32)
    for c in range(D_MODEL // sub):
        cols = slice(c * sub, (c + 1) * sub)
        h = x_ref[:, cols] + jnp.dot(m_ref[...], wb_ref[:, cols], preferred_element_type=F32)
        ssq = ssq + jnp.sum(h * h, axis=-1, keepdims=True)
        o_ref[:, cols] = h
    o_ref[...] = o_ref[...] * lax.rsqrt(ssq / D_MODEL + NORM_EPS) * fw_ref[...]


def _out_proj(merged, w_out, x2, final_w):
    tm = ROW_TILE
    return pl.pallas_call(
        _out_kernel,
        grid=(TOKENS // tm,),
        in_specs=[pl.BlockSpec((tm, D_MODEL), lambda i: (i, 0)),
                  pl.BlockSpec((None, D_MODEL, D_MODEL), lambda i: (0, 0, 0), pipeline_mode=pl.Buffered(1)),
                  pl.BlockSpec((tm, D_MODEL), lambda i: (i, 0)),
                  pl.BlockSpec((1, D_MODEL), lambda i: (0, 0))],
        out_specs=pl.BlockSpec((tm, D_MODEL), lambda i: (i, 0)),
        out_shape=jax.ShapeDtypeStruct((TOKENS, D_MODEL), F32),
        scratch_shapes=[pltpu.VMEM((D_MODEL, D_MODEL), BF16)],
        compiler_params=_params("arbitrary"),
        name="out_proj",
    )(merged, w_out, x2, final_w.reshape(1, D_MODEL))


def kernel(x, positions, norm_w, w_in, conv_w, conv_b, dt_bias, a_log, d_skip, ssd_norm_w,
           w_attn_br, w_ssd_br, gate_b, w_out, final_norm_w):
    assert x.shape == (BATCH, SEQ, D_MODEL) and w_in.shape[0] == 1
    x2 = x.reshape(TOKENS, D_MODEL)
    xn, cos, sin = _prep(x2, norm_w[0], positions)
    w_t = jnp.swapaxes(w_in, 1, 2)
    cw, cb = conv_w[0], conv_b.reshape(1, SSD_CONV_CH)
    p_main = _inproj(xn, w_t, cos, sin, cw, cb, _MAIN, "inproj_main")
    dil_a, dil_b = _inproj(xn, w_t, cos, sin, cw, cb, _DILATED, "inproj_dilated")
    gates, dt, dt_t = _gate_proj(xn, w_t, gate_b[0].reshape(1, 2 * D_MODEL), dt_bias[0])
    attn = _attention(p_main, dil_a, dil_b)
    ssd, ssd_rs = _ssd(p_main, dt, dt_t, a_log[0], d_skip[0], ssd_norm_w[0])
    merged = _branches(attn, ssd, ssd_rs, w_attn_br, w_ssd_br, gates)
    out = _out_proj(merged, w_out, x2, final_norm_w)
    return out.reshape(BATCH, SEQ, D_MODEL)
```
